```python
import jax, jax.numpy as jnp
from jax import lax
import numpy as np

D_MODEL = 4096
BATCH = 8
SEQ = 2048
DEPTH = 2
DEC_BATCH = 1
DEC_SEQ = 16384
PAST_LEN = 128

HEAD_DIM = 128
ATTN_WIDTH = D_MODEL // 2
N_Q_HEADS = ATTN_WIDTH // HEAD_DIM
N_KV_HEADS = N_Q_HEADS // 4
KV_WIDTH = N_KV_HEADS * HEAD_DIM
FOURIER_WIDTH = D_MODEL - ATTN_WIDTH
FOURIER_GROUP_DIM = 128
N_FOURIER_GROUPS = FOURIER_WIDTH // FOURIER_GROUP_DIM
MIX_WIDTH = ATTN_WIDTH + FOURIER_WIDTH
IN_PROJ_WIDTH = ATTN_WIDTH + 2 * KV_WIDTH + FOURIER_WIDTH
D_FF = -(-8 * D_MODEL // (3 * 256)) * 256
N_MOD = 6
GRID_W = 64
ROPE_THETA = 10000.0
Q_BLOCK = 128
EPS = 1e-6

kernel_name = "hymba_gqa_fnet_adaln_encoder"


def rms_norm(x, g):
    xf = x.astype(jnp.float32)
    y = xf * lax.rsqrt(jnp.mean(xf * xf, axis=-1, keepdims=True) + EPS)
    return (y * g.astype(jnp.float32)).astype(x.dtype)


def axial_rope_tables(seq):
    n_rows = seq // GRID_W
    row = jnp.repeat(jnp.arange(n_rows, dtype=jnp.float32), GRID_W)
    col = jnp.tile(jnp.arange(GRID_W, dtype=jnp.float32), n_rows)
    axis_dim = HEAD_DIM // 2
    inv_freq = 1.0 / (ROPE_THETA ** (jnp.arange(0, axis_dim, 2, dtype=jnp.float32) / axis_dim))
    ang_r = row[:, None] * inv_freq[None, :]
    ang_c = col[:, None] * inv_freq[None, :]
    return jnp.cos(ang_r), jnp.sin(ang_r), jnp.cos(ang_c), jnp.sin(ang_c)


def _rotate(x, cos, sin):
    x1, x2 = jnp.split(x, 2, axis=-1)
    cos = cos[None, :, None, :]
    sin = sin[None, :, None, :]
    return jnp.concatenate([x1 * cos - x2 * sin, x2 * cos + x1 * sin], axis=-1)


def apply_axial_rope(x, tables):
    cos_r, sin_r, cos_c, sin_c = tables
    xf = x.astype(jnp.float32)
    half = HEAD_DIM // 2
    out = jnp.concatenate([_rotate(xf[..., :half], cos_r, sin_r),
                           _rotate(xf[..., half:], cos_c, sin_c)], axis=-1)
    return out.astype(x.dtype)


def gqa_attention(q, k, v):
    b, s, hq, d = q.shape
    grp = hq // N_KV_HEADS
    nb = s // Q_BLOCK
    scale = d ** -0.5
    qb = q.reshape(b, nb, Q_BLOCK, N_KV_HEADS, grp, d).transpose(1, 0, 2, 3, 4, 5)

    def block(qi):
        sc = jnp.einsum('bqhgd,bkhd->bhgqk', qi, k).astype(jnp.float32) * scale
        p = jax.nn.softmax(sc, axis=-1).astype(v.dtype)
        return jnp.einsum('bhgqk,bkhd->bqhgd', p, v)

    out = lax.map(block, qb)
    return out.transpose(1, 0, 2, 3, 4, 5).reshape(b, s, hq * d)


def fourier_mix(u, w_f):
    b, s, _ = u.shape
    ug = u.reshape(b, s, N_FOURIER_GROUPS, FOURIER_GROUP_DIM).astype(jnp.float32)
    f = jnp.fft.fft2(ug, axes=(1, 3), norm="ortho").real.astype(u.dtype)
    return jnp.einsum('bsgc,gcd->bsgd', f, w_f).reshape(b, s, FOURIER_WIDTH)


def encoder_layer(x, c, l, ada_w, ada_b, norm_mix_g, w_in, q_norm_g, k_norm_g, w_fourier,
                  attn_out_g, fourier_out_g, w_out, norm_ffn_g, w_gate, w_up, w_down, rope):
    b, s, _ = x.shape
    mod = (jax.nn.silu(c) @ ada_w[l] + ada_b[l])[:, None, :]
    shift_a, scale_a, gate_a, shift_f, scale_f, gate_f = jnp.split(mod, N_MOD, axis=-1)

    h = rms_norm(x, norm_mix_g[l]) * (1.0 + scale_a) + shift_a
    proj = h @ w_in[l]
    q, k, v, u = jnp.split(proj, [ATTN_WIDTH, ATTN_WIDTH + KV_WIDTH, ATTN_WIDTH + 2 * KV_WIDTH], axis=-1)
    q = apply_axial_rope(rms_norm(q.reshape(b, s, N_Q_HEADS, HEAD_DIM), q_norm_g[l]), rope)
    k = apply_axial_rope(rms_norm(k.reshape(b, s, N_KV_HEADS, HEAD_DIM), k_norm_g[l]), rope)
    v = v.reshape(b, s, N_KV_HEADS, HEAD_DIM)
    a = gqa_attention(q, k, v)
    f = fourier_mix(u, w_fourier[l])
    mixed = jnp.concatenate([rms_norm(a, attn_out_g[l]), rms_norm(f, fourier_out_g[l])], axis=-1)
    x = x + gate_a * (mixed @ w_out[l])

    h = rms_norm(x, norm_ffn_g[l]) * (1.0 + scale_f) + shift_f
    ff = (jax.nn.silu(h @ w_gate[l]) * (h @ w_up[l])) @ w_down[l]
    return x + gate_f * ff


def trunk(x, c, ada_w, ada_b, norm_mix_g, w_in, q_norm_g, k_norm_g, w_fourier,
          attn_out_g, fourier_out_g, w_out, norm_ffn_g, w_gate, w_up, w_down, final_norm_g):
    rope = axial_rope_tables(x.shape[1])
    for l in range(DEPTH):
        x = encoder_layer(x, c, l, ada_w, ada_b, norm_mix_g, w_in, q_norm_g, k_norm_g, w_fourier,
                          attn_out_g, fourier_out_g, w_out, norm_ffn_g, w_gate, w_up, w_down, rope)
    return rms_norm(x, final_norm_g)


def setup_inputs(seed: int = 0) -> dict:
    key = jax.random.key(seed)
    ks = jax.random.split(key, 24)
    f32 = jnp.float32

    def nrm(k, shape, scale):
        return jax.random.normal(k, shape, f32) * scale

    def gain(k, shape):
        return 1.0 + 0.05 * jax.random.normal(k, shape, f32)

    return {
        "x_prompt": nrm(ks[0], (BATCH, SEQ, D_MODEL), 1.0),
        "x_sample": nrm(ks[1], (DEC_BATCH, DEC_SEQ, D_MODEL), 1.0),
        "c_prompt": nrm(ks[2], (BATCH, D_MODEL), 1.0),
        "c_sample": nrm(ks[3], (DEC_BATCH, D_MODEL), 1.0),
        "ada_w": nrm(ks[4], (DEPTH, D_MODEL, N_MOD * D_MODEL), D_MODEL ** -0.5),
        "ada_b": nrm(ks[5], (DEPTH, N_MOD * D_MODEL), 0.02),
        "norm_mix_g": gain(ks[6], (DEPTH, D_MODEL)),
        "w_in": nrm(ks[7], (DEPTH, D_MODEL, IN_PROJ_WIDTH), D_MODEL ** -0.5),
        "q_norm_g": gain(ks[8], (DEPTH, HEAD_DIM)),
        "k_norm_g": gain(ks[9], (DEPTH, HEAD_DIM)),
        "w_fourier": nrm(ks[10], (DEPTH, N_FOURIER_GROUPS, FOURIER_GROUP_DIM, FOURIER_GROUP_DIM), FOURIER_GROUP_DIM ** -0.5),
        "attn_out_g": gain(ks[11], (DEPTH, ATTN_WIDTH)),
        "fourier_out_g": gain(ks[12], (DEPTH, FOURIER_WIDTH)),
        "w_out": nrm(ks[13], (DEPTH, MIX_WIDTH, D_MODEL), MIX_WIDTH ** -0.5),
        "norm_ffn_g": gain(ks[14], (DEPTH, D_MODEL)),
        "w_gate": nrm(ks[15], (DEPTH, D_MODEL, D_FF), D_MODEL ** -0.5),
        "w_up": nrm(ks[16], (DEPTH, D_MODEL, D_FF), D_MODEL ** -0.5),
        "w_down": nrm(ks[17], (DEPTH, D_FF, D_MODEL), D_FF ** -0.5),
        "final_norm_g": gain(ks[18], (D_MODEL,)),
    }


def reference(x_prompt, x_sample, c_prompt, c_sample, ada_w, ada_b, norm_mix_g, w_in, q_norm_g,
              k_norm_g, w_fourier, attn_out_g, fourier_out_g, w_out, norm_ffn_g, w_gate, w_up,
              w_down, final_norm_g):
    y_prompt = trunk(x_prompt, c_prompt, ada_w, ada_b, norm_mix_g, w_in, q_norm_g, k_norm_g, w_fourier,
                     attn_out_g, fourier_out_g, w_out, norm_ffn_g, w_gate, w_up, w_down, final_norm_g)
    y_sample = trunk(x_sample, c_sample, ada_w, ada_b, norm_mix_g, w_in, q_norm_g, k_norm_g, w_fourier,
                     attn_out_g, fourier_out_g, w_out, norm_ffn_g, w_gate, w_up, w_down, final_norm_g)
    return (y_prompt, y_sample)
```

```python
import functools
import math

import numpy as np
import jax
import jax.numpy as jnp
from jax import lax
from jax.experimental import pallas as pl
from jax.experimental.pallas import tpu as pltpu

F32 = jnp.float32
BF16 = jnp.bfloat16

HEAD_DIM = 128
GQA_GROUP = 4
FOURIER_GROUP_DIM = 128
N_MOD = 6
GRID_W = 64
ROPE_THETA = 10000.0
EPS = 1e-6
FFT_INNER = 128

V7X_VMEM_BYTES = 64 * 1024 * 1024
V7X_VMEM_BUDGET = 56 * 1024 * 1024
LANES = 128


def _vmem_limit(pipelined_bytes, resident_bytes):
  need = 2 * pipelined_bytes + resident_bytes
  return int(min(V7X_VMEM_BUDGET, max(need, 16 * 1024 * 1024)))


def _params(semantics, pipelined_bytes, resident_bytes):
  return pltpu.CompilerParams(
      dimension_semantics=semantics,
      vmem_limit_bytes=_vmem_limit(pipelined_bytes, resident_bytes))


def _silu(x):
  return x / (1.0 + jnp.exp(-x))


def _adaln_kernel(c_ref, w_ref, b_ref, o_ref):
  sc = _silu(c_ref[...])
  o_ref[...] = jnp.dot(sc, w_ref[...], preferred_element_type=F32) + b_ref[...]


def _adaln(c_all, ada_w, ada_b):
  depth, d, n = ada_w.shape
  rows = c_all.shape[0]
  tn = 512
  return pl.pallas_call(
      _adaln_kernel,
      grid=(depth, n // tn),
      in_specs=[
          pl.BlockSpec((rows, d), lambda l, j: (0, 0)),
          pl.BlockSpec((None, d, tn), lambda l, j: (l, 0, j)),
          pl.BlockSpec((None, 1, tn), lambda l, j: (l, 0, j)),
      ],
      out_specs=pl.BlockSpec((None, rows, tn), lambda l, j: (l, 0, j)),
      out_shape=jax.ShapeDtypeStruct((depth, rows, n), F32),
      compiler_params=_params(("arbitrary", "arbitrary"), d * tn * 4, rows * d * 8),
      name="adaln_mod",
  )(c_all, ada_w, ada_b.reshape(depth, 1, n))


def _norm_mod_kernel(x_ref, g_ref, mod_ref, o_ref, *, shift_row, scale_row):
  x = x_ref[...]
  ms = jnp.mean(x * x, axis=-1, keepdims=True)
  y = x * lax.rsqrt(ms + EPS) * g_ref[...]
  scale = mod_ref[scale_row:scale_row + 1, :]
  shift = mod_ref[shift_row:shift_row + 1, :]
  o_ref[...] = (y * (1.0 + scale) + shift).astype(BF16)


def _norm_mod(x2, g, mod3, seq, b_off, shift_row, scale_row):
  m, d = x2.shape
  tr = 256
  return pl.pallas_call(
      functools.partial(_norm_mod_kernel, shift_row=shift_row, scale_row=scale_row),
      grid=(m // tr,),
      in_specs=[
          pl.BlockSpec((tr, d), lambda i: (i, 0)),
          pl.BlockSpec((1, d), lambda i: (0, 0)),
          pl.BlockSpec((None, N_MOD, d), lambda i: (b_off + (i * tr) // seq, 0, 0)),
      ],
      out_specs=pl.BlockSpec((tr, d), lambda i: (i, 0)),
      out_shape=jax.ShapeDtypeStruct((m, d), BF16),
      compiler_params=_params(("arbitrary",), tr * d * 6, 4 * tr * d * 4),
      name="norm_mod",
  )(x2, g.reshape(1, d), mod3)


def _final_norm_kernel(x_ref, g_ref, o_ref):
  x = x_ref[...]
  ms = jnp.mean(x * x, axis=-1, keepdims=True)
  o_ref[...] = x * lax.rsqrt(ms + EPS) * g_ref[...]


def _final_norm(x2, g):
  m, d = x2.shape
  tr = 256
  return pl.pallas_call(
      _final_norm_kernel,
      grid=(m // tr,),
      in_specs=[
          pl.BlockSpec((tr, d), lambda i: (i, 0)),
          pl.BlockSpec((1, d), lambda i: (0, 0)),
      ],
      out_specs=pl.BlockSpec((tr, d), lambda i: (i, 0)),
      out_shape=jax.ShapeDtypeStruct((m, d), F32),
      compiler_params=_params(("arbitrary",), tr * d * 8, 3 * tr * d * 4),
      name="final_norm",
  )(x2, g.reshape(1, d))


def _head_norm_rope(t, g, cos, sin_signed):
  ms = jnp.mean(t * t, axis=-1, keepdims=True)
  y = t * lax.rsqrt(ms + EPS) * g
  lane = lax.broadcasted_iota(jnp.int32, y.shape, 1)
  first_half = (lane & (HEAD_DIM // 4)) == 0
  partner = jnp.where(first_half,
                      pltpu.roll(y, HEAD_DIM - HEAD_DIM // 4, 1),
                      pltpu.roll(y, HEAD_DIM // 4, 1))
  return y * cos + partner * sin_signed


def _in_proj_kernel(h_ref, w_ref, cos_ref, sin_ref, qg_ref, kg_ref, o_ref, *,
                    n_q_blocks, q_scale):
  j = pl.program_id(1)
  acc = jnp.dot(h_ref[...], w_ref[...], preferred_element_type=F32)
  heads = acc.shape[1] // HEAD_DIM

  @pl.when(j < n_q_blocks)
  def _():
    for c in range(heads):
      sl = slice(c * HEAD_DIM, (c + 1) * HEAD_DIM)
      r = _head_norm_rope(acc[:, sl], qg_ref[...], cos_ref[...], sin_ref[...])
      o_ref[:, sl] = (r * q_scale).astype(BF16)

  @pl.when(j == n_q_blocks)
  def _():
    for c in range(heads):
      sl = slice(c * HEAD_DIM, (c + 1) * HEAD_DIM)
      r = _head_norm_rope(acc[:, sl], kg_ref[...], cos_ref[...], sin_ref[...])
      o_ref[:, sl] = r.astype(BF16)

  @pl.when(j > n_q_blocks)
  def _():
    o_ref[...] = acc.astype(BF16)


def _in_proj(h, w, cos, sin_signed, qg, kg, seq, attn_width, kv_width):
  m, d = h.shape
  n = w.shape[1]
  tm = min(1024, seq)
  tn = kv_width
  assert attn_width % tn == 0 and seq % tm == 0
  blocks_per_seq = seq // tm
  return pl.pallas_call(
      functools.partial(_in_proj_kernel, n_q_blocks=attn_width // tn,
                        q_scale=HEAD_DIM ** -0.5),
      grid=(m // tm, n // tn),
      in_specs=[
          pl.BlockSpec((tm, d), lambda i, j: (i, 0)),
          pl.BlockSpec((d, tn), lambda i, j: (0, j)),
          pl.BlockSpec((tm, HEAD_DIM), lambda i, j: (i % blocks_per_seq, 0)),
          pl.BlockSpec((tm, HEAD_DIM), lambda i, j: (i % blocks_per_seq, 0)),
          pl.BlockSpec((1, HEAD_DIM), lambda i, j: (0, 0)),
          pl.BlockSpec((1, HEAD_DIM), lambda i, j: (0, 0)),
      ],
      out_specs=pl.BlockSpec((tm, tn), lambda i, j: (i, j)),
      out_shape=jax.ShapeDtypeStruct((m, n), BF16),
      compiler_params=_params(
          ("arbitrary", "arbitrary"),
          tm * d * 2 + d * tn * 2 + tm * tn * 2 + 2 * tm * HEAD_DIM * 4,
          3 * tm * tn * 4),
      name="in_proj",
  )(h, w, cos, sin_signed, qg.reshape(1, HEAD_DIM), kg.reshape(1, HEAD_DIM))


def _attn_kernel(q_ref, k_ref, v_ref, o_ref, q4_ref, vt_ref, m_ref, l_ref, acc_ref,
                 *, tq, tk, vt_chunk):
  seq = k_ref.shape[0]

  @pl.when(pl.program_id(2) == 0)
  def _():
    for c in range(seq // vt_chunk):
      sl = slice(c * vt_chunk, (c + 1) * vt_chunk)
      vt_ref[:, sl] = v_ref[sl, :].astype(F32).T.astype(BF16)

  for g in range(GQA_GROUP):
    q4_ref[g * tq:(g + 1) * tq, :] = q_ref[:, g * HEAD_DIM:(g + 1) * HEAD_DIM]
  m_ref[...] = jnp.full(m_ref.shape, -jnp.inf, F32)
  l_ref[...] = jnp.zeros(l_ref.shape, F32)
  acc_ref[...] = jnp.zeros(acc_ref.shape, F32)

  def body(c, carry):
    start = pl.multiple_of(c * tk, tk)
    ks = k_ref[pl.ds(start, tk), :]
    s = lax.dot_general(ks, q4_ref[...], (((1,), (1,)), ((), ())),
                        preferred_element_type=F32)
    m_old = m_ref[...]
    m_new = jnp.maximum(m_old, jnp.max(s, axis=0, keepdims=True))
    alpha = jnp.exp(m_old - m_new)
    p = jnp.exp(s - m_new)
    l_ref[...] = alpha * l_ref[...] + jnp.sum(p, axis=0, keepdims=True)
    pv = jnp.dot(vt_ref[:, pl.ds(start, tk)], p.astype(BF16),
                 preferred_element_type=F32)
    acc_ref[...] = alpha * acc_ref[...] + pv
    m_ref[...] = m_new
    return carry

  lax.fori_loop(0, seq // tk, body, 0)
  out = acc_ref[...] / l_ref[...]
  for g in range(GQA_GROUP):
    o_ref[:, g * HEAD_DIM:(g + 1) * HEAD_DIM] = (
        out[:, g * tq:(g + 1) * tq].T.astype(BF16))


def _attention(proj3, attn_width, kv_width):
  b, seq, _ = proj3.shape
  n_kv = kv_width // HEAD_DIM
  tq = 256
  tk = min(512, seq)
  vt_chunk = min(512, seq)
  gw = GQA_GROUP * HEAD_DIM
  k_col0 = attn_width // HEAD_DIM
  v_col0 = (attn_width + kv_width) // HEAD_DIM
  return pl.pallas_call(
      functools.partial(_attn_kernel, tq=tq, tk=tk, vt_chunk=vt_chunk),
      grid=(b, n_kv, seq // tq),
      in_specs=[
          pl.BlockSpec((None, tq, gw), lambda bi, h, i: (bi, i, h)),
          pl.BlockSpec((None, seq, HEAD_DIM), lambda bi, h, i: (bi, 0, k_col0 + h)),
          pl.BlockSpec((None, seq, HEAD_DIM), lambda bi, h, i: (bi, 0, v_col0 + h)),
      ],
      out_specs=pl.BlockSpec((None, tq, gw), lambda bi, h, i: (bi, i, h)),
      out_shape=jax.ShapeDtypeStruct((b, seq, attn_width), BF16),
      scratch_shapes=[
          pltpu.VMEM((GQA_GROUP * tq, HEAD_DIM), BF16),
          pltpu.VMEM((HEAD_DIM, seq), BF16),
          pltpu.VMEM((1, GQA_GROUP * tq), F32),
          pltpu.VMEM((1, GQA_GROUP * tq), F32),
          pltpu.VMEM((HEAD_DIM, GQA_GROUP * tq), F32),
      ],
      compiler_params=_params(
          ("arbitrary", "arbitrary", "arbitrary"),
          2 * tq * gw * 2 + 2 * seq * HEAD_DIM * 2,
          seq * HEAD_DIM * 2 + 4 * tk * GQA_GROUP * tq * 4),
      name="attention",
  )(proj3, proj3, proj3)


def _dft_cos_sin(n):
  idx = np.arange(n, dtype=np.int64)
  ang = 2.0 * np.pi * ((idx[:, None] * idx[None, :]) % n).astype(np.float64) / n
  return np.cos(ang), np.sin(ang)


def _fourier_tables(seq):
  n1 = seq // FFT_INNER
  c2, s2 = _dft_cos_sin(FFT_INNER)
  f_inner = np.concatenate([c2, -s2], axis=0)
  k2 = np.arange(FFT_INNER, dtype=np.int64)
  i1 = np.arange(n1, dtype=np.int64)
  ang = 2.0 * np.pi * ((i1[:, None] * k2[None, :]) % seq).astype(np.float64) / seq
  tw_cos = np.repeat(np.cos(ang)[:, :, None], LANES, axis=2)
  tw_sin = np.repeat(np.sin(ang)[:, :, None], LANES, axis=2)
  c1, s1 = _dft_cos_sin(n1)
  f_outer = np.block([[c1, s1], [-s1, c1]])
  cc, sc = _dft_cos_sin(FOURIER_GROUP_DIM)
  f_chan = np.concatenate([cc, sc], axis=0)
  return (jnp.asarray(f_inner, BF16), jnp.asarray(tw_cos, F32), jnp.asarray(tw_sin, F32),
          jnp.asarray(f_outer, BF16), jnp.asarray(f_chan, BF16))


def _fft_stage1_kernel(u_ref, f_ref, tc_ref, ts_ref, zr_ref, zi_ref):
  y = jnp.dot(f_ref[...], u_ref[...], preferred_element_type=F32)
  yr = y[:FFT_INNER]
  yi = y[FFT_INNER:]
  tc = tc_ref[...]
  ts = ts_ref[...]
  for c in range(u_ref.shape[1] // LANES):
    sl = slice(c * LANES, (c + 1) * LANES)
    zr_ref[:, sl] = (yr[:, sl] * tc + yi[:, sl] * ts).astype(BF16)
    zi_ref[:, sl] = (yi[:, sl] * tc - yr[:, sl] * ts).astype(BF16)


def _fft_stage1(proj3, f_inner, tw_cos, tw_sin, u_col0, width):
  b, seq, n_proj = proj3.shape
  n1 = seq // FFT_INNER
  tc = 1024
  assert n_proj % tc == 0 and u_col0 % tc == 0 and width % tc == 0
  u_view = proj3.reshape(b, FFT_INNER, n1 * n_proj)
  cols_per_n1 = n_proj // tc
  col0 = u_col0 // tc
  z_shape = jax.ShapeDtypeStruct((b, n1, FFT_INNER, width), BF16)
  z_spec = pl.BlockSpec((None, None, FFT_INNER, tc), lambda bi, i, hh: (bi, i, 0, hh))
  return pl.pallas_call(
      _fft_stage1_kernel,
      grid=(b, n1, width // tc),
      in_specs=[
          pl.BlockSpec((None, FFT_INNER, tc),
                       lambda bi, i, hh: (bi, 0, i * cols_per_n1 + col0 + hh)),
          pl.BlockSpec((2 * FFT_INNER, FFT_INNER), lambda bi, i, hh: (0, 0)),
          pl.BlockSpec((None, FFT_INNER, LANES), lambda bi, i, hh: (i, 0, 0)),
          pl.BlockSpec((None, FFT_INNER, LANES), lambda bi, i, hh: (i, 0, 0)),
      ],
      out_specs=[z_spec, z_spec],
      out_shape=[z_shape, z_shape],
      compiler_params=_params(
          ("arbitrary", "arbitrary", "arbitrary"),
          3 * FFT_INNER * tc * 2 + 2 * FFT_INNER * LANES * 4,
          4 * 2 * FFT_INNER * tc * 4),
      name="fourier_stage1",
  )(u_view, f_inner, tw_cos, tw_sin)


def _fft_stage2_kernel(zr_ref, zi_ref, f_ref, fc_ref, wf_ref, o_ref, *, n1, kb, width,
                       scale):
  groups = width // FOURIER_GROUP_DIM
  o_re, o_im = [], []
  for kk in range(kb):
    sl = slice(kk * width, (kk + 1) * width)
    z = jnp.concatenate([zr_ref[:, sl], zi_ref[:, sl]], axis=0)
    o = jnp.dot(f_ref[...], z, preferred_element_type=F32)
    o_re.append(o[:n1])
    o_im.append(o[n1:])
  o_re = jnp.concatenate(o_re, axis=0).astype(BF16)
  o_im = jnp.concatenate(o_im, axis=0).astype(BF16)
  for g in range(groups):
    gs = slice(g * FOURIER_GROUP_DIM, (g + 1) * FOURIER_GROUP_DIM)
    a = jnp.concatenate([o_re[:, gs], o_im[:, gs]], axis=1)
    f = jnp.dot(a, fc_ref[...], preferred_element_type=F32) * scale
    r = jnp.dot(f.astype(BF16), wf_ref[g], preferred_element_type=F32).astype(BF16)
    for kk in range(kb):
      c0 = kk * width + g * FOURIER_GROUP_DIM
      o_ref[:, c0:c0 + FOURIER_GROUP_DIM] = r[kk * n1:(kk + 1) * n1, :]


def _fft_stage2(zr, zi, f_outer, f_chan, wf, seq):
  b, n1, _, width = zr.shape
  kb = max(1, FFT_INNER // n1)
  groups = width // FOURIER_GROUP_DIM
  zr2 = zr.reshape(b, n1, FFT_INNER * width)
  zi2 = zi.reshape(b, n1, FFT_INNER * width)
  blk = pl.BlockSpec((None, n1, kb * width), lambda bi, j: (bi, 0, j))
  out = pl.pallas_call(
      functools.partial(_fft_stage2_kernel, n1=n1, kb=kb, width=width,
                        scale=1.0 / math.sqrt(seq * FOURIER_GROUP_DIM)),
      grid=(b, FFT_INNER // kb),
      in_specs=[
          blk, blk,
          pl.BlockSpec((2 * n1, 2 * n1), lambda bi, j: (0, 0)),
          pl.BlockSpec((2 * FOURIER_GROUP_DIM, FOURIER_GROUP_DIM), lambda bi, j: (0, 0)),
          pl.BlockSpec((groups, FOURIER_GROUP_DIM, FOURIER_GROUP_DIM),
                       lambda bi, j: (0, 0, 0)),
      ],
      out_specs=blk,
      out_shape=jax.ShapeDtypeStruct((b, n1, FFT_INNER * width), BF16),
      compiler_params=_params(
          ("arbitrary", "arbitrary"),
          3 * n1 * kb * width * 2 + 4 * n1 * n1 * 2 + width * FOURIER_GROUP_DIM * 2,
          6 * n1 * kb * width * 4),
      name="fourier_stage2",
  )(zr2, zi2, f_outer, f_chan, wf)
  return out.reshape(b * seq, width)


def _out_proj_kernel(a_ref, f_ref, ag_ref, fg_ref, w_ref, x_ref, mod_ref, o_ref,
                     mixed_ref, *, gate_row, row_chunk):
  wa = a_ref.shape[1]

  @pl.when(pl.program_id(1) == 0)
  def _():
    def body(r, carry):
      rows = pl.ds(pl.multiple_of(r * row_chunk, row_chunk), row_chunk)
      a = a_ref[rows, :].astype(F32)
      ya = a * lax.rsqrt(jnp.mean(a * a, axis=-1, keepdims=True) + EPS) * ag_ref[...]
      mixed_ref[rows, :wa] = ya.astype(BF16)
      f = f_ref[rows, :].astype(F32)
      yf = f * lax.rsqrt(jnp.mean(f * f, axis=-1, keepdims=True) + EPS) * fg_ref[...]
      mixed_ref[rows, wa:] = yf.astype(BF16)
      return carry
    lax.fori_loop(0, a_ref.shape[0] // row_chunk, body, 0)

  acc = jnp.dot(mixed_ref[...], w_ref[...], preferred_element_type=F32)
  o_ref[...] = x_ref[...] + mod_ref[gate_row:gate_row + 1, :] * acc


def _out_proj(a2, f2, ag, fg, w, x2, mod3, seq, b_off, gate_row):
  m, wa = a2.shape
  wf = f2.shape[1]
  d = w.shape[1]
  tm = min(1024, seq)
  tn = 512
  return pl.pallas_call(
      functools.partial(_out_proj_kernel, gate_row=gate_row, row_chunk=128),
      grid=(m // tm, d // tn),
      in_specs=[
          pl.BlockSpec((tm, wa), lambda i, j: (i, 0)),
          pl.BlockSpec((tm, wf), lambda i, j: (i, 0)),
          pl.BlockSpec((1, wa), lambda i, j: (0, 0)),
          pl.BlockSpec((1, wf), lambda i, j: (0, 0)),
          pl.BlockSpec((wa + wf, tn), lambda i, j: (0, j)),
          pl.BlockSpec((tm, tn), lambda i, j: (i, j)),
          pl.BlockSpec((None, N_MOD, tn), lambda i, j: (b_off + (i * tm) // seq, 0, j)),
      ],
      out_specs=pl.BlockSpec((tm, tn), lambda i, j: (i, j)),
      out_shape=jax.ShapeDtypeStruct((m, d), F32),
      scratch_shapes=[pltpu.VMEM((tm, wa + wf), BF16)],
      compiler_params=_params(
          ("arbitrary", "arbitrary"),
          tm * (wa + wf) * 2 + (wa + wf) * tn * 2 + 2 * tm * tn * 4,
          tm * (wa + wf) * 2 + 2 * tm * tn * 4),
      name="out_proj",
  )(a2, f2, ag.reshape(1, wa), fg.reshape(1, wf), w, x2, mod3)


def _ffn_up_kernel(h_ref, wg_ref, wu_ref, o_ref):
  h = h_ref[...]
  g = jnp.dot(h, wg_ref[...], preferred_element_type=F32)
  u = jnp.dot(h, wu_ref[...], preferred_element_type=F32)
  o_ref[...] = (_silu(g) * u).astype(BF16)


def _ffn_up(h, wg, wu):
  m, d = h.shape
  f = wg.shape[1]
  tm = min(1024, m)
  tf = 256
  assert m % tm == 0 and f % tf == 0
  return pl.pallas_call(
      _ffn_up_kernel,
      grid=(m // tm, f // tf),
      in_specs=[
          pl.BlockSpec((tm, d), lambda i, j: (i, 0)),
          pl.BlockSpec((d, tf), lambda i, j: (0, j)),
          pl.BlockSpec((d, tf), lambda i, j: (0, j)),
      ],
      out_specs=pl.BlockSpec((tm, tf), lambda i, j: (i, j)),
      out_shape=jax.ShapeDtypeStruct((m, f), BF16),
      compiler_params=_params(
          ("arbitrary", "arbitrary"),
          tm * d * 2 + 2 * d * tf * 2 + tm * tf * 2,
          4 * tm * tf * 4),
      name="ffn_up",
  )(h, wg, wu)


def _ffn_down_kernel(a_ref, w_ref, x_ref, mod_ref, o_ref, *, gate_row):
  acc = jnp.dot(a_ref[...], w_ref[...], preferred_element_type=F32)
  o_ref[...] = x_ref[...] + mod_ref[gate_row:gate_row + 1, :] * acc


def _ffn_down(act, w, x2, mod3, seq, b_off, gate_row):
  m, f = act.shape
  d = w.shape[1]
  tm = min(512, seq)
  tn = 512
  assert m % tm == 0 and seq % tm == 0
  return pl.pallas_call(
      functools.partial(_ffn_down_kernel, gate_row=gate_row),
      grid=(m // tm, d // tn),
      in_specs=[
          pl.BlockSpec((tm, f), lambda i, j: (i, 0)),
          pl.BlockSpec((f, tn), lambda i, j: (0, j)),
          pl.BlockSpec((tm, tn), lambda i, j: (i, j)),
          pl.BlockSpec((None, N_MOD, tn), lambda i, j: (b_off + (i * tm) // seq, 0, j)),
      ],
      out_specs=pl.BlockSpec((tm, tn), lambda i, j: (i, j)),
      out_shape=jax.ShapeDtypeStruct((m, d), F32),
      compiler_params=_params(
          ("arbitrary", "arbitrary"),
          tm * f * 2 + f * tn * 2 + 2 * tm * tn * 4,
          2 * tm * tn * 4),
      name="ffn_down",
  )(act, w, x2, mod3)


def _rope_tables(seq):
  t = np.arange(seq)
  row = (t // GRID_W).astype(np.float32)
  col = (t % GRID_W).astype(np.float32)
  axis_dim = HEAD_DIM // 2
  inv_freq = (1.0 / (ROPE_THETA ** (np.arange(0, axis_dim, 2, dtype=np.float32) / axis_dim))
              ).astype(np.float32)
  ang_r = row[:, None] * inv_freq[None, :]
  ang_c = col[:, None] * inv_freq[None, :]
  cos = np.concatenate([np.cos(ang_r), np.cos(ang_r), np.cos(ang_c), np.cos(ang_c)], axis=1)
  sin = np.concatenate([-np.sin(ang_r), np.sin(ang_r), -np.sin(ang_c), np.sin(ang_c)], axis=1)
  return jnp.asarray(cos, F32), jnp.asarray(sin, F32)


def _trunk(x, mods, b_off, weights, final_g):
  (norm_mix_g, w_in, q_norm_g, k_norm_g, w_fourier, attn_out_g, fourier_out_g, w_out,
   norm_ffn_g, w_gate, w_up, w_down) = weights
  b, seq, d = x.shape
  depth = w_in.shape[0]
  attn_width = attn_out_g.shape[1]
  fourier_width = fourier_out_g.shape[1]
  kv_width = (w_in.shape[2] - attn_width - fourier_width) // 2
  cos, sin_signed = _rope_tables(seq)
  fft_tables = _fourier_tables(seq)
  f_inner, tw_cos, tw_sin, f_outer, f_chan = fft_tables
  x2 = x.reshape(b * seq, d)
  for l in range(depth):
    mod3 = mods[l]
    h = _norm_mod(x2, norm_mix_g[l], mod3, seq, b_off, shift_row=0, scale_row=1)
    proj = _in_proj(h, w_in[l], cos, sin_signed, q_norm_g[l], k_norm_g[l], seq,
                    attn_width, kv_width)
    proj3 = proj.reshape(b, seq, proj.shape[1])
    a = _attention(proj3, attn_width, kv_width)
    zr, zi = _fft_stage1(proj3, f_inner, tw_cos, tw_sin, attn_width + 2 * kv_width,
                         fourier_width)
    fm = _fft_stage2(zr, zi, f_outer, f_chan, w_fourier[l], seq)
    x2 = _out_proj(a.reshape(b * seq, attn_width), fm, attn_out_g[l], fourier_out_g[l],
                   w_out[l], x2, mod3, seq, b_off, gate_row=2)
    h = _norm_mod(x2, norm_ffn_g[l], mod3, seq, b_off, shift_row=3, scale_row=4)
    act = _ffn_up(h, w_gate[l], w_up[l])
    x2 = _ffn_down(act, w_down[l], x2, mod3, seq, b_off, gate_row=5)
  return _final_norm(x2, final_g).reshape(b, seq, d)


def kernel(x_prompt, x_sample, c_prompt, c_sample, ada_w, ada_b, norm_mix_g, w_in, q_norm_g,
           k_norm_g, w_fourier, attn_out_g, fourier_out_g, w_out, norm_ffn_g, w_gate, w_up,
           w_down, final_norm_g):
  depth, d, _ = ada_w.shape
  nb_prompt = c_prompt.shape[0]
  nb = nb_prompt + c_sample.shape[0]
  rows = -(-nb // 8) * 8
  c_all = jnp.concatenate(
      [c_prompt, c_sample, jnp.zeros((rows - nb, d), F32)], axis=0)
  mod_all = _adaln(c_all, ada_w, ada_b)
  mods = [mod_all[l].reshape(rows, N_MOD, d) for l in range(depth)]
  weights = (norm_mix_g, w_in.astype(BF16), q_norm_g, k_norm_g, w_fourier.astype(BF16),
             attn_out_g, fourier_out_g, w_out.astype(BF16), norm_ffn_g,
             w_gate.astype(BF16), w_up.astype(BF16), w_down.astype(BF16))
  y_prompt = _trunk(x_prompt, mods, 0, weights, final_norm_g)
  y_sample = _trunk(x_sample, mods, nb_prompt, weights, final_norm_g)
  return (y_prompt, y_sample)
```

```python
import functools
import math

import numpy as np
import jax
import jax.numpy as jnp
from jax import lax
from jax.experimental import pallas as pl
from jax.experimental.pallas import tpu as pltpu

F32 = jnp.float32
BF16 = jnp.bfloat16

HEAD_DIM = 128
GQA_GROUP = 4
FOURIER_GROUP_DIM = 128
N_MOD = 6
GRID_W = 64
ROPE_THETA = 10000.0
EPS = 1e-6
FFT_INNER = 128
FOURIER_DENSE_MAX_SEQ = 2048

V7X_VMEM_BYTES = 64 * 1024 * 1024
V7X_VMEM_BUDGET = 56 * 1024 * 1024
LANES = 128
SUBLANES = 8


def _vmem_limit(pipelined_bytes, resident_bytes):
  need = 2 * pipelined_bytes + resident_bytes
  return int(min(V7X_VMEM_BUDGET, max(need, 16 * 1024 * 1024)))


def _params(semantics, pipelined_bytes, resident_bytes):
  return pltpu.CompilerParams(
      dimension_semantics=semantics,
      vmem_limit_bytes=_vmem_limit(pipelined_bytes, resident_bytes))


def _silu(x):
  return x / (1.0 + jnp.exp(-x))


def _adaln_kernel(c_ref, w_ref, b_ref, o_ref):
  sc = _silu(c_ref[...])
  o_ref[...] = jnp.dot(sc, w_ref[...], preferred_element_type=F32) + b_ref[...]


def _adaln(c_all, ada_w, ada_b):
  depth, d, n = ada_w.shape
  rows = c_all.shape[0]
  tn = 512
  return pl.pallas_call(
      _adaln_kernel,
      grid=(depth, n // tn),
      in_specs=[
          pl.BlockSpec((rows, d), lambda l, j: (0, 0)),
          pl.BlockSpec((None, d, tn), lambda l, j: (l, 0, j)),
          pl.BlockSpec((None, 1, tn), lambda l, j: (l, 0, j)),
      ],
      out_specs=pl.BlockSpec((None, rows, tn), lambda l, j: (l, 0, j)),
      out_shape=jax.ShapeDtypeStruct((depth, rows, n), F32),
      compiler_params=_params(("arbitrary", "arbitrary"), d * tn * 4, rows * d * 8),
      name="adaln_mod",
  )(c_all, ada_w, ada_b.reshape(depth, 1, n))


def _norm_mod_kernel(x_ref, g_ref, mod_ref, o_ref, *, shift_row, scale_row):
  x = x_ref[...]
  ms = jnp.mean(x * x, axis=-1, keepdims=True)
  y = x * lax.rsqrt(ms + EPS) * g_ref[...]
  scale = mod_ref[scale_row:scale_row + 1, :]
  shift = mod_ref[shift_row:shift_row + 1, :]
  o_ref[...] = (y * (1.0 + scale) + shift).astype(BF16)


def _norm_mod(x2, g, mod3, seq, b_off, shift_row, scale_row):
  m, d = x2.shape
  tr = 256
  return pl.pallas_call(
      functools.partial(_norm_mod_kernel, shift_row=shift_row, scale_row=scale_row),
      grid=(m // tr,),
      in_specs=[
          pl.BlockSpec((tr, d), lambda i: (i, 0)),
          pl.BlockSpec((1, d), lambda i: (0, 0)),
          pl.BlockSpec((None, N_MOD, d), lambda i: (b_off + (i * tr) // seq, 0, 0)),
      ],
      out_specs=pl.BlockSpec((tr, d), lambda i: (i, 0)),
      out_shape=jax.ShapeDtypeStruct((m, d), BF16),
      compiler_params=_params(("arbitrary",), tr * d * 6, 4 * tr * d * 4),
      name="norm_mod",
  )(x2, g.reshape(1, d), mod3)


def _final_norm_kernel(x_ref, g_ref, o_ref):
  x = x_ref[...]
  ms = jnp.mean(x * x, axis=-1, keepdims=True)
  o_ref[...] = x * lax.rsqrt(ms + EPS) * g_ref[...]


def _final_norm(x2, g):
  m, d = x2.shape
  tr = 256
  return pl.pallas_call(
      _final_norm_kernel,
      grid=(m // tr,),
      in_specs=[
          pl.BlockSpec((tr, d), lambda i: (i, 0)),
          pl.BlockSpec((1, d), lambda i: (0, 0)),
      ],
      out_specs=pl.BlockSpec((tr, d), lambda i: (i, 0)),
      out_shape=jax.ShapeDtypeStruct((m, d), F32),
      compiler_params=_params(("arbitrary",), tr * d * 8, 3 * tr * d * 4),
      name="final_norm",
  )(x2, g.reshape(1, d))


def _head_norm_rope(t, g, cos, sin_signed):
  ms = jnp.mean(t * t, axis=-1, keepdims=True)
  y = t * lax.rsqrt(ms + EPS) * g
  lane = lax.broadcasted_iota(jnp.int32, y.shape, 1)
  first_half = (lane & (HEAD_DIM // 4)) == 0
  partner = jnp.where(first_half,
                      pltpu.roll(y, HEAD_DIM - HEAD_DIM // 4, 1),
                      pltpu.roll(y, HEAD_DIM // 4, 1))
  return y * cos + partner * sin_signed


def _in_proj_kernel(h_ref, w_ref, cos_ref, sin_ref, qg_ref, kg_ref, o_ref, *,
                    n_q_blocks, q_scale):
  j = pl.program_id(1)
  acc = jnp.dot(h_ref[...], w_ref[...], preferred_element_type=F32)
  heads = acc.shape[1] // HEAD_DIM

  @pl.when(j < n_q_blocks)
  def _():
    for c in range(heads):
      sl = slice(c * HEAD_DIM, (c + 1) * HEAD_DIM)
      r = _head_norm_rope(acc[:, sl], qg_ref[...], cos_ref[...], sin_ref[...])
      o_ref[:, sl] = (r * q_scale).astype(BF16)

  @pl.when(j == n_q_blocks)
  def _():
    for c in range(heads):
      sl = slice(c * HEAD_DIM, (c + 1) * HEAD_DIM)
      r = _head_norm_rope(acc[:, sl], kg_ref[...], cos_ref[...], sin_ref[...])
      o_ref[:, sl] = r.astype(BF16)

  @pl.when(j > n_q_blocks)
  def _():
    o_ref[...] = acc.astype(BF16)


def _in_proj(h, w, cos, sin_signed, qg, kg, seq, attn_width, kv_width):
  m, d = h.shape
  n = w.shape[1]
  tm = min(1024, seq)
  tn = kv_width
  assert attn_width % tn == 0 and seq % tm == 0
  blocks_per_seq = seq // tm
  return pl.pallas_call(
      functools.partial(_in_proj_kernel, n_q_blocks=attn_width // tn,
                        q_scale=HEAD_DIM ** -0.5 * math.log2(math.e)),
      grid=(m // tm, n // tn),
      in_specs=[
          pl.BlockSpec((tm, d), lambda i, j: (i, 0)),
          pl.BlockSpec((d, tn), lambda i, j: (0, j)),
          pl.BlockSpec((tm, HEAD_DIM), lambda i, j: (i % blocks_per_seq, 0)),
          pl.BlockSpec((tm, HEAD_DIM), lambda i, j: (i % blocks_per_seq, 0)),
          pl.BlockSpec((1, HEAD_DIM), lambda i, j: (0, 0)),
          pl.BlockSpec((1, HEAD_DIM), lambda i, j: (0, 0)),
      ],
      out_specs=pl.BlockSpec((tm, tn), lambda i, j: (i, j)),
      out_shape=jax.ShapeDtypeStruct((m, n), BF16),
      compiler_params=_params(
          ("arbitrary", "arbitrary"),
          tm * d * 2 + d * tn * 2 + tm * tn * 2 + 2 * tm * HEAD_DIM * 4,
          3 * tm * tn * 4),
      name="in_proj",
  )(h, w, cos, sin_signed, qg.reshape(1, HEAD_DIM), kg.reshape(1, HEAD_DIM))


def _attn_kernel(q_ref, k_ref, v_ref, o_ref, q4_ref, vt_ref, s0_ref, s1_ref, p0_ref,
                 p1_ref, m_ref, l_ref, acc_ref, *, tq, tk, vt_chunk):
  seq = k_ref.shape[0]
  nk = seq // tk
  assert nk % 2 == 0 and nk >= 2

  @pl.when(pl.program_id(2) == 0)
  def _():
    for c in range(seq // vt_chunk):
      sl = slice(c * vt_chunk, (c + 1) * vt_chunk)
      vt_ref[:, sl] = v_ref[sl, :].astype(F32).T.astype(BF16)

  for g in range(GQA_GROUP):
    q4_ref[g * tq:(g + 1) * tq, :] = q_ref[:, g * HEAD_DIM:(g + 1) * HEAD_DIM]
  m_ref[...] = jnp.full(m_ref.shape, -jnp.inf, F32)
  l_ref[...] = jnp.zeros(l_ref.shape, F32)
  acc_ref[...] = jnp.zeros(acc_ref.shape, F32)

  def scores(c, s_ref):
    ks = k_ref[pl.ds(pl.multiple_of(c * tk, tk), tk), :]
    s_ref[...] = lax.dot_general(ks, q4_ref[...], (((1,), (1,)), ((), ())),
                                 preferred_element_type=F32)

  def softmax(s_ref, p_ref):
    col_max = s_ref[0:SUBLANES, :]
    for r in range(1, tk // SUBLANES):
      col_max = jnp.maximum(col_max, s_ref[r * SUBLANES:(r + 1) * SUBLANES, :])
    m_old = m_ref[...]
    m_new = jnp.maximum(m_old, jnp.max(col_max, axis=0, keepdims=True))
    alpha = jnp.exp2(m_old - m_new)
    col_sum = jnp.zeros((SUBLANES, s_ref.shape[1]), F32)
    strip = 2 * SUBLANES
    for r in range(tk // strip):
      rows = slice(r * strip, (r + 1) * strip)
      p = jnp.exp2(s_ref[rows, :] - m_new)
      col_sum = col_sum + p[:SUBLANES] + p[SUBLANES:]
      p_ref[rows, :] = p.astype(BF16)
    l_ref[...] = alpha * l_ref[...] + jnp.sum(col_sum, axis=0, keepdims=True)
    m_ref[...] = m_new
    return alpha

  def weighted_values(c, p_ref, alpha):
    vt = vt_ref[:, pl.ds(pl.multiple_of(c * tk, tk), tk)]
    pv = jnp.dot(vt, p_ref[...], preferred_element_type=F32)
    acc_ref[...] = alpha * acc_ref[...] + pv

  scores(0, s0_ref)
  scores(1, s1_ref)
  alpha0 = softmax(s0_ref, p0_ref)

  def body(t, alpha_even):
    c = 2 * t + 1
    scores(c + 1, s0_ref)
    alpha_odd = softmax(s1_ref, p1_ref)
    weighted_values(c - 1, p0_ref, alpha_even)
    scores(c + 2, s1_ref)
    alpha_next = softmax(s0_ref, p0_ref)
    weighted_values(c, p1_ref, alpha_odd)
    return alpha_next

  alpha_even = lax.fori_loop(0, nk // 2 - 1, body, alpha0)
  alpha_odd = softmax(s1_ref, p1_ref)
  weighted_values(nk - 2, p0_ref, alpha_even)
  weighted_values(nk - 1, p1_ref, alpha_odd)
  out = acc_ref[...] / l_ref[...]
  for g in range(GQA_GROUP):
    o_ref[:, g * HEAD_DIM:(g + 1) * HEAD_DIM] = (
        out[:, g * tq:(g + 1) * tq].T.astype(BF16))


def _attention(proj3, attn_width, kv_width):
  b, seq, _ = proj3.shape
  n_kv = kv_width // HEAD_DIM
  tq = 256
  tk = min(512, seq // 2)
  vt_chunk = min(512, seq)
  n_lanes = GQA_GROUP * tq
  gw = GQA_GROUP * HEAD_DIM
  k_col0 = attn_width // HEAD_DIM
  v_col0 = (attn_width + kv_width) // HEAD_DIM
  return pl.pallas_call(
      functools.partial(_attn_kernel, tq=tq, tk=tk, vt_chunk=vt_chunk),
      grid=(b, n_kv, seq // tq),
      in_specs=[
          pl.BlockSpec((None, tq, gw), lambda bi, h, i: (bi, i, h)),
          pl.BlockSpec((None, seq, HEAD_DIM), lambda bi, h, i: (bi, 0, k_col0 + h)),
          pl.BlockSpec((None, seq, HEAD_DIM), lambda bi, h, i: (bi, 0, v_col0 + h)),
      ],
      out_specs=pl.BlockSpec((None, tq, gw), lambda bi, h, i: (bi, i, h)),
      out_shape=jax.ShapeDtypeStruct((b, seq, attn_width), BF16),
      scratch_shapes=[
          pltpu.VMEM((n_lanes, HEAD_DIM), BF16),
          pltpu.VMEM((HEAD_DIM, seq), BF16),
          pltpu.VMEM((tk, n_lanes), F32),
          pltpu.VMEM((tk, n_lanes), F32),
          pltpu.VMEM((tk, n_lanes), BF16),
          pltpu.VMEM((tk, n_lanes), BF16),
          pltpu.VMEM((1, n_lanes), F32),
          pltpu.VMEM((1, n_lanes), F32),
          pltpu.VMEM((HEAD_DIM, n_lanes), F32),
      ],
      compiler_params=_params(
          ("arbitrary", "arbitrary", "arbitrary"),
          2 * tq * gw * 2 + 2 * seq * HEAD_DIM * 2,
          seq * HEAD_DIM * 2 + 3 * tk * n_lanes * 4 + 4 * tk * n_lanes * 4),
      name="attention",
  )(proj3, proj3, proj3)


def _dft_cos_sin(n):
  idx = np.arange(n, dtype=np.int64)
  ang = 2.0 * np.pi * ((idx[:, None] * idx[None, :]) % n).astype(np.float64) / n
  return np.cos(ang), np.sin(ang)


def _angle_cos_sin(int_prod, n):
  ang = int_prod.astype(F32) * (2.0 * math.pi / n)
  return jnp.cos(ang), jnp.sin(ang)


def _fourier_tables(seq):
  n1 = seq // FFT_INNER
  c2, s2 = _dft_cos_sin(FFT_INNER)
  f_inner = np.concatenate([c2, -s2], axis=0)
  k2 = jnp.arange(FFT_INNER, dtype=jnp.int32)
  i1 = jnp.arange(n1, dtype=jnp.int32)
  tw_c, tw_s = _angle_cos_sin((i1[:, None] * k2[None, :]) % seq, seq)
  tw_cos = jnp.broadcast_to(tw_c[:, :, None], (n1, FFT_INNER, LANES))
  tw_sin = jnp.broadcast_to(tw_s[:, :, None], (n1, FFT_INNER, LANES))
  c1, s1 = _dft_cos_sin(n1)
  f_outer = np.block([[c1, s1], [-s1, c1]])
  cc, sc = _dft_cos_sin(FOURIER_GROUP_DIM)
  f_chan = np.concatenate([cc, sc], axis=0)
  return (jnp.asarray(f_inner, BF16), tw_cos, tw_sin,
          jnp.asarray(f_outer, BF16), jnp.asarray(f_chan, BF16))


def _fourier_dense_kernel(u_ref, fc_ref, cpos_ref, spos_ref, wf_ref, o_ref, uc_ref, us_ref,
                          *, scale):
  groups = u_ref.shape[1] // FOURIER_GROUP_DIM

  @pl.when(pl.program_id(2) == 0)
  def _():
    for g in range(groups):
      gs = slice(g * FOURIER_GROUP_DIM, (g + 1) * FOURIER_GROUP_DIM)
      y = jnp.dot(u_ref[:, gs], fc_ref[...], preferred_element_type=F32)
      uc_ref[:, gs] = y[:, :FOURIER_GROUP_DIM].astype(BF16)
      us_ref[:, gs] = y[:, FOURIER_GROUP_DIM:].astype(BF16)

  f = (jnp.dot(cpos_ref[...], uc_ref[...], preferred_element_type=F32)
       - jnp.dot(spos_ref[...], us_ref[...], preferred_element_type=F32))
  f = (f * scale).astype(BF16)
  for g in range(groups):
    gs = slice(g * FOURIER_GROUP_DIM, (g + 1) * FOURIER_GROUP_DIM)
    o_ref[:, gs] = jnp.dot(f[:, gs], wf_ref[g], preferred_element_type=F32).astype(BF16)


def _fourier_dense(proj3, wf, u_col0, width):
  b, seq, _ = proj3.shape
  tn = 1024
  ts = min(512, seq)
  assert u_col0 % tn == 0 and width % tn == 0 and seq % ts == 0
  groups = tn // FOURIER_GROUP_DIM
  idx = jnp.arange(seq, dtype=jnp.int32)
  cpos, spos = _angle_cos_sin((idx[:, None] * idx[None, :]) % seq, seq)
  cc, sc = _dft_cos_sin(FOURIER_GROUP_DIM)
  f_chan = jnp.asarray(np.concatenate([cc, sc], axis=1), BF16)
  return pl.pallas_call(
      functools.partial(_fourier_dense_kernel,
                        scale=1.0 / math.sqrt(seq * FOURIER_GROUP_DIM)),
      grid=(b, width // tn, seq // ts),
      in_specs=[
          pl.BlockSpec((None, seq, tn), lambda bi, j, i: (bi, 0, u_col0 // tn + j)),
          pl.BlockSpec((FOURIER_GROUP_DIM, 2 * FOURIER_GROUP_DIM), lambda bi, j, i: (0, 0)),
          pl.BlockSpec((ts, seq), lambda bi, j, i: (i, 0)),
          pl.BlockSpec((ts, seq), lambda bi, j, i: (i, 0)),
          pl.BlockSpec((groups, FOURIER_GROUP_DIM, FOURIER_GROUP_DIM),
                       lambda bi, j, i: (j, 0, 0)),
      ],
      out_specs=pl.BlockSpec((None, ts, tn), lambda bi, j, i: (bi, i, j)),
      out_shape=jax.ShapeDtypeStruct((b, seq, width), BF16),
      scratch_shapes=[pltpu.VMEM((seq, tn), BF16), pltpu.VMEM((seq, tn), BF16)],
      compiler_params=_params(
          ("arbitrary", "arbitrary", "arbitrary"),
          seq * tn * 2 + 2 * ts * seq * 2 + ts * tn * 2,
          2 * seq * tn * 2 + 2 * seq * 2 * FOURIER_GROUP_DIM * 4 + 3 * ts * tn * 4),
      name="fourier_dense",
  )(proj3, f_chan, cpos.astype(BF16), spos.astype(BF16), wf)


def _fft_stage1_kernel(u_ref, f_ref, tc_ref, ts_ref, zr_ref, zi_ref):
  y = jnp.dot(f_ref[...], u_ref[...], preferred_element_type=F32)
  yr = y[:FFT_INNER]
  yi = y[FFT_INNER:]
  tc = tc_ref[...]
  ts = ts_ref[...]
  for c in range(u_ref.shape[1] // LANES):
    sl = slice(c * LANES, (c + 1) * LANES)
    zr_ref[:, sl] = (yr[:, sl] * tc + yi[:, sl] * ts).astype(BF16)
    zi_ref[:, sl] = (yi[:, sl] * tc - yr[:, sl] * ts).astype(BF16)


def _fft_stage1(u3, f_inner, tw_cos, tw_sin):
  b, seq, width = u3.shape
  n1 = seq // FFT_INNER
  tc = 1024
  assert width % tc == 0
  u_view = u3.reshape(b, FFT_INNER, n1 * width)
  cols_per_n1 = width // tc
  col0 = 0
  z_shape = jax.ShapeDtypeStruct((b, n1, FFT_INNER, width), BF16)
  z_spec = pl.BlockSpec((None, None, FFT_INNER, tc), lambda bi, i, hh: (bi, i, 0, hh))
  return pl.pallas_call(
      _fft_stage1_kernel,
      grid=(b, n1, width // tc),
      in_specs=[
          pl.BlockSpec((None, FFT_INNER, tc),
                       lambda bi, i, hh: (bi, 0, i * cols_per_n1 + col0 + hh)),
          pl.BlockSpec((2 * FFT_INNER, FFT_INNER), lambda bi, i, hh: (0, 0)),
          pl.BlockSpec((None, FFT_INNER, LANES), lambda bi, i, hh: (i, 0, 0)),
          pl.BlockSpec((None, FFT_INNER, LANES), lambda bi, i, hh: (i, 0, 0)),
      ],
      out_specs=[z_spec, z_spec],
      out_shape=[z_shape, z_shape],
      compiler_params=_params(
          ("arbitrary", "arbitrary", "arbitrary"),
          3 * FFT_INNER * tc * 2 + 2 * FFT_INNER * LANES * 4,
          4 * 2 * FFT_INNER * tc * 4),
      name="fourier_stage1",
  )(u_view, f_inner, tw_cos, tw_sin)


def _fft_stage2_kernel(zr_ref, zi_ref, f_ref, fc_ref, wf_ref, o_ref, *, n1, kb, width,
                       scale):
  groups = width // FOURIER_GROUP_DIM
  o_re, o_im = [], []
  for kk in range(kb):
    sl = slice(kk * width, (kk + 1) * width)
    z = jnp.concatenate([zr_ref[:, sl], zi_ref[:, sl]], axis=0)
    o = jnp.dot(f_ref[...], z, preferred_element_type=F32)
    o_re.append(o[:n1])
    o_im.append(o[n1:])
  o_re = jnp.concatenate(o_re, axis=0).astype(BF16)
  o_im = jnp.concatenate(o_im, axis=0).astype(BF16)
  for g in range(groups):
    gs = slice(g * FOURIER_GROUP_DIM, (g + 1) * FOURIER_GROUP_DIM)
    a = jnp.concatenate([o_re[:, gs], o_im[:, gs]], axis=1)
    f = jnp.dot(a, fc_ref[...], preferred_element_type=F32) * scale
    r = jnp.dot(f.astype(BF16), wf_ref[g], preferred_element_type=F32).astype(BF16)
    for kk in range(kb):
      c0 = kk * width + g * FOURIER_GROUP_DIM
      o_ref[:, c0:c0 + FOURIER_GROUP_DIM] = r[kk * n1:(kk + 1) * n1, :]


def _fft_stage2(zr, zi, f_outer, f_chan, wf, seq):
  b, n1, _, width = zr.shape
  kb = max(1, FFT_INNER // n1)
  groups = width // FOURIER_GROUP_DIM
  zr2 = zr.reshape(b, n1, FFT_INNER * width)
  zi2 = zi.reshape(b, n1, FFT_INNER * width)
  blk = pl.BlockSpec((None, n1, kb * width), lambda bi, j: (bi, 0, j))
  out = pl.pallas_call(
      functools.partial(_fft_stage2_kernel, n1=n1, kb=kb, width=width,
                        scale=1.0 / math.sqrt(seq * FOURIER_GROUP_DIM)),
      grid=(b, FFT_INNER // kb),
      in_specs=[
          blk, blk,
          pl.BlockSpec((2 * n1, 2 * n1), lambda bi, j: (0, 0)),
          pl.BlockSpec((2 * FOURIER_GROUP_DIM, FOURIER_GROUP_DIM), lambda bi, j: (0, 0)),
          pl.BlockSpec((groups, FOURIER_GROUP_DIM, FOURIER_GROUP_DIM),
                       lambda bi, j: (0, 0, 0)),
      ],
      out_specs=blk,
      out_shape=jax.ShapeDtypeStruct((b, n1, FFT_INNER * width), BF16),
      compiler_params=_params(
          ("arbitrary", "arbitrary"),
          3 * n1 * kb * width * 2 + 4 * n1 * n1 * 2 + width * FOURIER_GROUP_DIM * 2,
          6 * n1 * kb * width * 4),
      name="fourier_stage2",
  )(zr2, zi2, f_outer, f_chan, wf)
  return out.reshape(b * seq, width)


def _out_proj_kernel(a_ref, f_ref, ag_ref, fg_ref, w_ref, x_ref, mod_ref, o_ref,
                     mixed_ref, *, gate_row, row_chunk):
  wa = a_ref.shape[1]

  @pl.when(pl.program_id(1) == 0)
  def _():
    def body(r, carry):
      rows = pl.ds(pl.multiple_of(r * row_chunk, row_chunk), row_chunk)
      a = a_ref[rows, :].astype(F32)
      ya = a * lax.rsqrt(jnp.mean(a * a, axis=-1, keepdims=True) + EPS) * ag_ref[...]
      mixed_ref[rows, :wa] = ya.astype(BF16)
      f = f_ref[rows, :].astype(F32)
      yf = f * lax.rsqrt(jnp.mean(f * f, axis=-1, keepdims=True) + EPS) * fg_ref[...]
      mixed_ref[rows, wa:] = yf.astype(BF16)
      return carry
    lax.fori_loop(0, a_ref.shape[0] // row_chunk, body, 0)

  acc = jnp.dot(mixed_ref[...], w_ref[...], preferred_element_type=F32)
  o_ref[...] = x_ref[...] + mod_ref[gate_row:gate_row + 1, :] * acc


def _out_proj(a2, f2, ag, fg, w, x2, mod3, seq, b_off, gate_row):
  m, wa = a2.shape
  wf = f2.shape[1]
  d = w.shape[1]
  tm = min(1024, seq)
  tn = 512
  return pl.pallas_call(
      functools.partial(_out_proj_kernel, gate_row=gate_row, row_chunk=128),
      grid=(m // tm, d // tn),
      in_specs=[
          pl.BlockSpec((tm, wa), lambda i, j: (i, 0)),
          pl.BlockSpec((tm, wf), lambda i, j: (i, 0)),
          pl.BlockSpec((1, wa), lambda i, j: (0, 0)),
          pl.BlockSpec((1, wf), lambda i, j: (0, 0)),
          pl.BlockSpec((wa + wf, tn), lambda i, j: (0, j)),
          pl.BlockSpec((tm, tn), lambda i, j: (i, j)),
          pl.BlockSpec((None, N_MOD, tn), lambda i, j: (b_off + (i * tm) // seq, 0, j)),
      ],
      out_specs=pl.BlockSpec((tm, tn), lambda i, j: (i, j)),
      out_shape=jax.ShapeDtypeStruct((m, d), F32),
      scratch_shapes=[pltpu.VMEM((tm, wa + wf), BF16)],
      compiler_params=_params(
          ("arbitrary", "arbitrary"),
          tm * (wa + wf) * 2 + (wa + wf) * tn * 2 + 2 * tm * tn * 4,
          tm * (wa + wf) * 2 + 2 * tm * tn * 4),
      name="out_proj",
  )(a2, f2, ag.reshape(1, wa), fg.reshape(1, wf), w, x2, mod3)


def _ffn_up_kernel(h_ref, wg_ref, wu_ref, o_ref):
  h = h_ref[...]
  g = jnp.dot(h, wg_ref[...], preferred_element_type=F32)
  u = jnp.dot(h, wu_ref[...], preferred_element_type=F32)
  o_ref[...] = (_silu(g) * u).astype(BF16)


def _ffn_up(h, wg, wu):
  m, d = h.shape
  f = wg.shape[1]
  tm = min(1024, m)
  tf = 256
  assert m % tm == 0 and f % tf == 0
  return pl.pallas_call(
      _ffn_up_kernel,
      grid=(m // tm, f // tf),
      in_specs=[
          pl.BlockSpec((tm, d), lambda i, j: (i, 0)),
          pl.BlockSpec((d, tf), lambda i, j: (0, j)),
          pl.BlockSpec((d, tf), lambda i, j: (0, j)),
      ],
      out_specs=pl.BlockSpec((tm, tf), lambda i, j: (i, j)),
      out_shape=jax.ShapeDtypeStruct((m, f), BF16),
      compiler_params=_params(
          ("arbitrary", "arbitrary"),
          tm * d * 2 + 2 * d * tf * 2 + tm * tf * 2,
          4 * tm * tf * 4),
      name="ffn_up",
  )(h, wg, wu)


def _ffn_down_kernel(a_ref, w_ref, x_ref, mod_ref, o_ref, *, gate_row):
  acc = jnp.dot(a_ref[...], w_ref[...], preferred_element_type=F32)
  o_ref[...] = x_ref[...] + mod_ref[gate_row:gate_row + 1, :] * acc


def _ffn_down(act, w, x2, mod3, seq, b_off, gate_row):
  m, f = act.shape
  d = w.shape[1]
  tm = min(512, seq)
  tn = 512
  assert m % tm == 0 and seq % tm == 0
  return pl.pallas_call(
      functools.partial(_ffn_down_kernel, gate_row=gate_row),
      grid=(m // tm, d // tn),
      in_specs=[
          pl.BlockSpec((tm, f), lambda i, j: (i, 0)),
          pl.BlockSpec((f, tn), lambda i, j: (0, j)),
          pl.BlockSpec((tm, tn), lambda i, j: (i, j)),
          pl.BlockSpec((None, N_MOD, tn), lambda i, j: (b_off + (i * tm) // seq, 0, j)),
      ],
      out_specs=pl.BlockSpec((tm, tn), lambda i, j: (i, j)),
      out_shape=jax.ShapeDtypeStruct((m, d), F32),
      compiler_params=_params(
          ("arbitrary", "arbitrary"),
          tm * f * 2 + f * tn * 2 + 2 * tm * tn * 4,
          2 * tm * tn * 4),
      name="ffn_down",
  )(act, w, x2, mod3)


def _rope_tables(seq):
  t = np.arange(seq)
  row = (t // GRID_W).astype(np.float32)
  col = (t % GRID_W).astype(np.float32)
  axis_dim = HEAD_DIM // 2
  inv_freq = (1.0 / (ROPE_THETA ** (np.arange(0, axis_dim, 2, dtype=np.float32) / axis_dim))
              ).astype(np.float32)
  ang_r = row[:, None] * inv_freq[None, :]
  ang_c = col[:, None] * inv_freq[None, :]
  cos = np.concatenate([np.cos(ang_r), np.cos(ang_r), np.cos(ang_c), np.cos(ang_c)], axis=1)
  sin = np.concatenate([-np.sin(ang_r), np.sin(ang_r), -np.sin(ang_c), np.sin(ang_c)], axis=1)
  return jnp.asarray(cos, F32), jnp.asarray(sin, F32)


def _trunk(x, mods, b_off, weights, final_g):
  (norm_mix_g, w_in, q_norm_g, k_norm_g, w_fourier, attn_out_g, fourier_out_g, w_out,
   norm_ffn_g, w_gate, w_up, w_down) = weights
  b, seq, d = x.shape
  depth = w_in.shape[0]
  attn_width = attn_out_g.shape[1]
  fourier_width = fourier_out_g.shape[1]
  kv_width = (w_in.shape[2] - attn_width - fourier_width) // 2
  cos, sin_signed = _rope_tables(seq)
  u_col0 = attn_width + 2 * kv_width
  dense_fourier = seq <= FOURIER_DENSE_MAX_SEQ
  if not dense_fourier:
    f_inner, tw_cos, tw_sin, f_outer, f_chan = _fourier_tables(seq)
  x2 = x.reshape(b * seq, d)
  for l in range(depth):
    mod3 = mods[l]
    h = _norm_mod(x2, norm_mix_g[l], mod3, seq, b_off, shift_row=0, scale_row=1)
    proj = _in_proj(h, w_in[l], cos, sin_signed, q_norm_g[l], k_norm_g[l], seq,
                    attn_width, kv_width)
    proj3 = proj.reshape(b, seq, proj.shape[1])
    a = _attention(proj3, attn_width, kv_width)
    if dense_fourier:
      fm = _fourier_dense(proj3, w_fourier[l], u_col0, fourier_width)
      fm = fm.reshape(b * seq, fourier_width)
    else:
      zr, zi = _fft_stage1(proj3[:, :, u_col0:], f_inner, tw_cos, tw_sin)
      fm = _fft_stage2(zr, zi, f_outer, f_chan, w_fourier[l], seq)
    x2 = _out_proj(a.reshape(b * seq, attn_width), fm, attn_out_g[l], fourier_out_g[l],
                   w_out[l], x2, mod3, seq, b_off, gate_row=2)
    h = _norm_mod(x2, norm_ffn_g[l], mod3, seq, b_off, shift_row=3, scale_row=4)
    act = _ffn_up(h, w_gate[l], w_up[l])
    x2 = _ffn_down(act, w_down[l], x2, mod3, seq, b_off, gate_row=5)
  return _final_norm(x2, final_g).reshape(b, seq, d)


def kernel(x_prompt, x_sample, c_prompt, c_sample, ada_w, ada_b, norm_mix_g, w_in, q_norm_g,
           k_norm_g, w_fourier, attn_out_g, fourier_out_g, w_out, norm_ffn_g, w_gate, w_up,
           w_down, final_norm_g):
  depth, d, _ = ada_w.shape
  nb_prompt = c_prompt.shape[0]
  nb = nb_prompt + c_sample.shape[0]
  rows = -(-nb // 8) * 8
  c_all = jnp.concatenate(
      [c_prompt, c_sample, jnp.zeros((rows - nb, d), F32)], axis=0)
  mod_all = _adaln(c_all, ada_w, ada_b)
  mods = [mod_all[l].reshape(rows, N_MOD, d) for l in range(depth)]
  weights = (norm_mix_g, w_in.astype(BF16), q_norm_g, k_norm_g, w_fourier.astype(BF16),
             attn_out_g, fourier_out_g, w_out.astype(BF16), norm_ffn_g,
             w_gate.astype(BF16), w_up.astype(BF16), w_down.astype(BF16))
  y_prompt = _trunk(x_prompt, mods, 0, weights, final_norm_g)
  y_sample = _trunk(x_sample, mods, nb_prompt, weights, final_norm_g)
  return (y_prompt, y_sample)
```

```python
import functools
import math

import numpy as np
import jax
import jax.numpy as jnp
from jax import lax
from jax.experimental import pallas as pl
from jax.experimental.pallas import tpu as pltpu

F32 = jnp.float32
BF16 = jnp.bfloat16

HEAD_DIM = 128
GQA_GROUP = 4
FOURIER_GROUP_DIM = 128
N_MOD = 6
GRID_W = 64
ROPE_THETA = 10000.0
EPS = 1e-6
FFT_INNER = 128
FFT_TILE = 16
FOURIER_DENSE_MAX_SEQ = 2048

V7X_VMEM_BYTES = 64 * 1024 * 1024
V7X_VMEM_BUDGET = 56 * 1024 * 1024
LANES = 128
SUBLANES = 8


def _vmem_limit(pipelined_bytes, resident_bytes):
  need = 2 * pipelined_bytes + resident_bytes
  return int(min(V7X_VMEM_BUDGET, max(need, 16 * 1024 * 1024)))


def _params(semantics, pipelined_bytes, resident_bytes):
  return pltpu.CompilerParams(
      dimension_semantics=semantics,
      vmem_limit_bytes=_vmem_limit(pipelined_bytes, resident_bytes))


def _silu(x):
  return x / (1.0 + jnp.exp(-x))


def _adaln_kernel(c_ref, w_ref, b_ref, o_ref):
  sc = _silu(c_ref[...])
  o_ref[...] = jnp.dot(sc, w_ref[...], preferred_element_type=F32) + b_ref[...]


def _adaln(c_all, ada_w, ada_b):
  depth, d, n = ada_w.shape
  rows = c_all.shape[0]
  tn = 512
  return pl.pallas_call(
      _adaln_kernel,
      grid=(depth, n // tn),
      in_specs=[
          pl.BlockSpec((rows, d), lambda l, j: (0, 0)),
          pl.BlockSpec((None, d, tn), lambda l, j: (l, 0, j)),
          pl.BlockSpec((None, 1, tn), lambda l, j: (l, 0, j)),
      ],
      out_specs=pl.BlockSpec((None, rows, tn), lambda l, j: (l, 0, j)),
      out_shape=jax.ShapeDtypeStruct((depth, rows, n), F32),
      compiler_params=_params(("arbitrary", "arbitrary"), d * tn * 4, rows * d * 8),
      name="adaln_mod",
  )(c_all, ada_w, ada_b.reshape(depth, 1, n))


def _norm_mod_kernel(x_ref, g_ref, mod_ref, o_ref, *, shift_row, scale_row):
  x = x_ref[...]
  ms = jnp.mean(x * x, axis=-1, keepdims=True)
  y = x * lax.rsqrt(ms + EPS) * g_ref[...]
  scale = mod_ref[scale_row:scale_row + 1, :]
  shift = mod_ref[shift_row:shift_row + 1, :]
  o_ref[...] = (y * (1.0 + scale) + shift).astype(BF16)


def _norm_mod(x2, g, mod3, seq, b_off, shift_row, scale_row):
  m, d = x2.shape
  tr = 256
  return pl.pallas_call(
      functools.partial(_norm_mod_kernel, shift_row=shift_row, scale_row=scale_row),
      grid=(m // tr,),
      in_specs=[
          pl.BlockSpec((tr, d), lambda i: (i, 0)),
          pl.BlockSpec((1, d), lambda i: (0, 0)),
          pl.BlockSpec((None, N_MOD, d), lambda i: (b_off + (i * tr) // seq, 0, 0)),
      ],
      out_specs=pl.BlockSpec((tr, d), lambda i: (i, 0)),
      out_shape=jax.ShapeDtypeStruct((m, d), BF16),
      compiler_params=_params(("arbitrary",), tr * d * 6, 4 * tr * d * 4),
      name="norm_mod",
  )(x2, g.reshape(1, d), mod3)


def _final_norm_kernel(x_ref, g_ref, o_ref):
  x = x_ref[...]
  ms = jnp.mean(x * x, axis=-1, keepdims=True)
  o_ref[...] = x * lax.rsqrt(ms + EPS) * g_ref[...]


def _final_norm(x2, g):
  m, d = x2.shape
  tr = 256
  return pl.pallas_call(
      _final_norm_kernel,
      grid=(m // tr,),
      in_specs=[
          pl.BlockSpec((tr, d), lambda i: (i, 0)),
          pl.BlockSpec((1, d), lambda i: (0, 0)),
      ],
      out_specs=pl.BlockSpec((tr, d), lambda i: (i, 0)),
      out_shape=jax.ShapeDtypeStruct((m, d), F32),
      compiler_params=_params(("arbitrary",), tr * d * 8, 3 * tr * d * 4),
      name="final_norm",
  )(x2, g.reshape(1, d))


def _head_norm_rope(t, g, cos, sin_signed):
  ms = jnp.mean(t * t, axis=-1, keepdims=True)
  y = t * lax.rsqrt(ms + EPS) * g
  lane = lax.broadcasted_iota(jnp.int32, y.shape, 1)
  first_half = (lane & (HEAD_DIM // 4)) == 0
  partner = jnp.where(first_half,
                      pltpu.roll(y, HEAD_DIM - HEAD_DIM // 4, 1),
                      pltpu.roll(y, HEAD_DIM // 4, 1))
  return y * cos + partner * sin_signed


def _in_proj_kernel(h_ref, w_ref, cos_ref, sin_ref, qg_ref, kg_ref, o_ref, *,
                    n_q_blocks, q_scale):
  j = pl.program_id(1)
  acc = jnp.dot(h_ref[...], w_ref[...], preferred_element_type=F32)
  heads = acc.shape[1] // HEAD_DIM

  @pl.when(j < n_q_blocks)
  def _():
    for c in range(heads):
      sl = slice(c * HEAD_DIM, (c + 1) * HEAD_DIM)
      r = _head_norm_rope(acc[:, sl], qg_ref[...], cos_ref[...], sin_ref[...])
      o_ref[:, sl] = (r * q_scale).astype(BF16)

  @pl.when(j == n_q_blocks)
  def _():
    for c in range(heads):
      sl = slice(c * HEAD_DIM, (c + 1) * HEAD_DIM)
      r = _head_norm_rope(acc[:, sl], kg_ref[...], cos_ref[...], sin_ref[...])
      o_ref[:, sl] = r.astype(BF16)

  @pl.when(j > n_q_blocks)
  def _():
    o_ref[...] = acc.astype(BF16)


def _in_proj(h, w, layer, cos, sin_signed, qg, kg, seq, attn_width, kv_width):
  m, d = h.shape
  n = w.shape[2]
  tm = min(1024, seq)
  tn = kv_width
  assert attn_width % tn == 0 and seq % tm == 0
  blocks_per_seq = seq // tm
  return pl.pallas_call(
      functools.partial(_in_proj_kernel, n_q_blocks=attn_width // tn,
                        q_scale=HEAD_DIM ** -0.5 * math.log2(math.e)),
      grid=(m // tm, n // tn),
      in_specs=[
          pl.BlockSpec((tm, d), lambda i, j: (i, 0)),
          pl.BlockSpec((None, d, tn), lambda i, j: (layer, 0, j)),
          pl.BlockSpec((tm, HEAD_DIM), lambda i, j: (i % blocks_per_seq, 0)),
          pl.BlockSpec((tm, HEAD_DIM), lambda i, j: (i % blocks_per_seq, 0)),
          pl.BlockSpec((1, HEAD_DIM), lambda i, j: (0, 0)),
          pl.BlockSpec((1, HEAD_DIM), lambda i, j: (0, 0)),
      ],
      out_specs=pl.BlockSpec((tm, tn), lambda i, j: (i, j)),
      out_shape=jax.ShapeDtypeStruct((m, n), BF16),
      compiler_params=_params(
          ("arbitrary", "arbitrary"),
          tm * d * 2 + d * tn * 2 + tm * tn * 2 + 2 * tm * HEAD_DIM * 4,
          3 * tm * tn * 4),
      name="in_proj",
  )(h, w, cos, sin_signed, qg.reshape(1, HEAD_DIM), kg.reshape(1, HEAD_DIM))


def _attn_kernel(q_ref, k_ref, v_ref, o_ref, q4_ref, vt_ref, s0_ref, s1_ref, p0_ref,
                 p1_ref, m_ref, l_ref, acc_ref, *, tq, tk, vt_chunk):
  seq = k_ref.shape[0]
  nk = seq // tk
  assert nk % 2 == 0 and nk >= 2

  @pl.when(pl.program_id(2) == 0)
  def _():
    for c in range(seq // vt_chunk):
      sl = slice(c * vt_chunk, (c + 1) * vt_chunk)
      vt_ref[:, sl] = v_ref[sl, :].astype(F32).T.astype(BF16)

  for g in range(GQA_GROUP):
    q4_ref[g * tq:(g + 1) * tq, :] = q_ref[:, g * HEAD_DIM:(g + 1) * HEAD_DIM]
  m_ref[...] = jnp.full(m_ref.shape, -jnp.inf, F32)
  l_ref[...] = jnp.zeros(l_ref.shape, F32)
  acc_ref[...] = jnp.zeros(acc_ref.shape, F32)

  def scores(c, s_ref):
    ks = k_ref[pl.ds(pl.multiple_of(c * tk, tk), tk), :]
    s_ref[...] = lax.dot_general(ks, q4_ref[...], (((1,), (1,)), ((), ())),
                                 preferred_element_type=F32)

  def softmax(s_ref, p_ref):
    col_max = s_ref[0:SUBLANES, :]
    for r in range(1, tk // SUBLANES):
      col_max = jnp.maximum(col_max, s_ref[r * SUBLANES:(r + 1) * SUBLANES, :])
    m_old = m_ref[...]
    m_new = jnp.maximum(m_old, jnp.max(col_max, axis=0, keepdims=True))
    alpha = jnp.exp2(m_old - m_new)
    col_sum = jnp.zeros((SUBLANES, s_ref.shape[1]), F32)
    strip = 2 * SUBLANES
    for r in range(tk // strip):
      rows = slice(r * strip, (r + 1) * strip)
      p = jnp.exp2(s_ref[rows, :] - m_new)
      col_sum = col_sum + p[:SUBLANES] + p[SUBLANES:]
      p_ref[rows, :] = p.astype(BF16)
    l_ref[...] = alpha * l_ref[...] + jnp.sum(col_sum, axis=0, keepdims=True)
    m_ref[...] = m_new
    return alpha

  def weighted_values(c, p_ref, alpha):
    vt = vt_ref[:, pl.ds(pl.multiple_of(c * tk, tk), tk)]
    pv = jnp.dot(vt, p_ref[...], preferred_element_type=F32)
    acc_ref[...] = alpha * acc_ref[...] + pv

  scores(0, s0_ref)
  scores(1, s1_ref)
  alpha0 = softmax(s0_ref, p0_ref)

  def body(t, alpha_even):
    c = 2 * t + 1
    scores(c + 1, s0_ref)
    alpha_odd = softmax(s1_ref, p1_ref)
    weighted_values(c - 1, p0_ref, alpha_even)
    scores(c + 2, s1_ref)
    alpha_next = softmax(s0_ref, p0_ref)
    weighted_values(c, p1_ref, alpha_odd)
    return alpha_next

  alpha_even = lax.fori_loop(0, nk // 2 - 1, body, alpha0)
  alpha_odd = softmax(s1_ref, p1_ref)
  weighted_values(nk - 2, p0_ref, alpha_even)
  weighted_values(nk - 1, p1_ref, alpha_odd)
  out = acc_ref[...] / l_ref[...]
  for g in range(GQA_GROUP):
    o_ref[:, g * HEAD_DIM:(g + 1) * HEAD_DIM] = (
        out[:, g * tq:(g + 1) * tq].T.astype(BF16))


def _attention(proj3, attn_width, kv_width):
  b, seq, _ = proj3.shape
  n_kv = kv_width // HEAD_DIM
  tq = 256
  tk = min(512, seq // 2)
  vt_chunk = min(512, seq)
  n_lanes = GQA_GROUP * tq
  gw = GQA_GROUP * HEAD_DIM
  k_col0 = attn_width // HEAD_DIM
  v_col0 = (attn_width + kv_width) // HEAD_DIM
  return pl.pallas_call(
      functools.partial(_attn_kernel, tq=tq, tk=tk, vt_chunk=vt_chunk),
      grid=(b, n_kv, seq // tq),
      in_specs=[
          pl.BlockSpec((None, tq, gw), lambda bi, h, i: (bi, i, h)),
          pl.BlockSpec((None, seq, HEAD_DIM), lambda bi, h, i: (bi, 0, k_col0 + h)),
          pl.BlockSpec((None, seq, HEAD_DIM), lambda bi, h, i: (bi, 0, v_col0 + h)),
      ],
      out_specs=pl.BlockSpec((None, tq, gw), lambda bi, h, i: (bi, i, h)),
      out_shape=jax.ShapeDtypeStruct((b, seq, attn_width), BF16),
      scratch_shapes=[
          pltpu.VMEM((n_lanes, HEAD_DIM), BF16),
          pltpu.VMEM((HEAD_DIM, seq), BF16),
          pltpu.VMEM((tk, n_lanes), F32),
          pltpu.VMEM((tk, n_lanes), F32),
          pltpu.VMEM((tk, n_lanes), BF16),
          pltpu.VMEM((tk, n_lanes), BF16),
          pltpu.VMEM((1, n_lanes), F32),
          pltpu.VMEM((1, n_lanes), F32),
          pltpu.VMEM((HEAD_DIM, n_lanes), F32),
      ],
      compiler_params=_params(
          ("arbitrary", "arbitrary", "arbitrary"),
          2 * tq * gw * 2 + 2 * seq * HEAD_DIM * 2,
          seq * HEAD_DIM * 2 + 3 * tk * n_lanes * 4 + 4 * tk * n_lanes * 4),
      name="attention",
  )(proj3, proj3, proj3)


def _dft_cos_sin(n):
  idx = np.arange(n, dtype=np.int64)
  ang = 2.0 * np.pi * ((idx[:, None] * idx[None, :]) % n).astype(np.float64) / n
  return np.cos(ang), np.sin(ang)


def _angle_cos_sin(int_prod, n):
  ang = int_prod.astype(F32) * (2.0 * math.pi / n)
  return jnp.cos(ang), jnp.sin(ang)


def _fourier_tables(seq):
  n1 = seq // FFT_INNER
  c2, s2 = _dft_cos_sin(FFT_INNER)
  f_inner = np.concatenate([c2, -s2], axis=0)
  k2 = jnp.arange(FFT_INNER, dtype=jnp.int32)
  i1 = jnp.arange(n1, dtype=jnp.int32)
  tw_c, tw_s = _angle_cos_sin((i1[:, None] * k2[None, :]) % seq, seq)
  tw_cos = jnp.broadcast_to(tw_c[:, :, None], (n1, FFT_INNER, LANES))
  tw_sin = jnp.broadcast_to(tw_s[:, :, None], (n1, FFT_INNER, LANES))
  c1, s1 = _dft_cos_sin(n1)
  f_outer = np.block([[c1, s1], [-s1, c1]])
  cc, sc = _dft_cos_sin(FOURIER_GROUP_DIM)
  f_chan = np.concatenate([cc, sc], axis=0)
  return (jnp.asarray(f_inner, BF16), tw_cos, tw_sin,
          jnp.asarray(f_outer, BF16), jnp.asarray(f_chan, BF16))


def _fourier_dense_kernel(u_ref, fc_ref, cpos_ref, spos_ref, wf_ref, o_ref, uc_ref, us_ref,
                          *, scale):
  groups = u_ref.shape[1] // FOURIER_GROUP_DIM

  @pl.when(pl.program_id(2) == 0)
  def _():
    for g in range(groups):
      gs = slice(g * FOURIER_GROUP_DIM, (g + 1) * FOURIER_GROUP_DIM)
      y = jnp.dot(u_ref[:, gs], fc_ref[...], preferred_element_type=F32)
      uc_ref[:, gs] = y[:, :FOURIER_GROUP_DIM].astype(BF16)
      us_ref[:, gs] = y[:, FOURIER_GROUP_DIM:].astype(BF16)

  f = (jnp.dot(cpos_ref[...], uc_ref[...], preferred_element_type=F32)
       - jnp.dot(spos_ref[...], us_ref[...], preferred_element_type=F32))
  f = (f * scale).astype(BF16)
  for g in range(groups):
    gs = slice(g * FOURIER_GROUP_DIM, (g + 1) * FOURIER_GROUP_DIM)
    o_ref[:, gs] = jnp.dot(f[:, gs], wf_ref[g], preferred_element_type=F32).astype(BF16)


def _fourier_dense(proj3, wf, layer, u_col0, width):
  b, seq, _ = proj3.shape
  tn = 1024
  ts = min(512, seq)
  assert u_col0 % tn == 0 and width % tn == 0 and seq % ts == 0
  groups = tn // FOURIER_GROUP_DIM
  idx = jnp.arange(seq, dtype=jnp.int32)
  cpos, spos = _angle_cos_sin((idx[:, None] * idx[None, :]) % seq, seq)
  cc, sc = _dft_cos_sin(FOURIER_GROUP_DIM)
  f_chan = jnp.asarray(np.concatenate([cc, sc], axis=1), BF16)
  return pl.pallas_call(
      functools.partial(_fourier_dense_kernel,
                        scale=1.0 / math.sqrt(seq * FOURIER_GROUP_DIM)),
      grid=(b, width // tn, seq // ts),
      in_specs=[
          pl.BlockSpec((None, seq, tn), lambda bi, j, i: (bi, 0, u_col0 // tn + j)),
          pl.BlockSpec((FOURIER_GROUP_DIM, 2 * FOURIER_GROUP_DIM), lambda bi, j, i: (0, 0)),
          pl.BlockSpec((ts, seq), lambda bi, j, i: (i, 0)),
          pl.BlockSpec((ts, seq), lambda bi, j, i: (i, 0)),
          pl.BlockSpec((None, groups, FOURIER_GROUP_DIM, FOURIER_GROUP_DIM),
                       lambda bi, j, i: (layer, j, 0, 0)),
      ],
      out_specs=pl.BlockSpec((None, ts, tn), lambda bi, j, i: (bi, i, j)),
      out_shape=jax.ShapeDtypeStruct((b, seq, width), BF16),
      scratch_shapes=[pltpu.VMEM((seq, tn), BF16), pltpu.VMEM((seq, tn), BF16)],
      compiler_params=_params(
          ("arbitrary", "arbitrary", "arbitrary"),
          seq * tn * 2 + 2 * ts * seq * 2 + ts * tn * 2,
          2 * seq * tn * 2 + 2 * seq * 2 * FOURIER_GROUP_DIM * 4 + 3 * ts * tn * 4),
      name="fourier_dense",
  )(proj3, f_chan, cpos.astype(BF16), spos.astype(BF16), wf)


def _fft_stage1_kernel(u_ref, f_ref, tc_ref, ts_ref, zr_ref, zi_ref):
  x = pltpu.einshape("abc->bac", u_ref[...])
  reps = u_ref.shape[2] // LANES
  for t in range(FFT_TILE):
    y = jnp.dot(f_ref[...], x[t], preferred_element_type=F32)
    yr = y[:FFT_INNER]
    yi = y[FFT_INNER:]
    tc = jnp.tile(tc_ref[t], (1, reps))
    ts = jnp.tile(ts_ref[t], (1, reps))
    zr_ref[t] = (yr * tc + yi * ts).astype(BF16)
    zi_ref[t] = (yi * tc - yr * ts).astype(BF16)


def _fft_stage1(proj3, f_inner, tw_cos, tw_sin, u_col0, width):
  b, seq, n_proj = proj3.shape
  n1 = seq // FFT_INNER
  tc = 1024
  assert n1 % FFT_TILE == 0 and u_col0 % tc == 0 and width % tc == 0
  u_view = proj3.reshape(b, FFT_INNER, n1, n_proj)
  z_shape = jax.ShapeDtypeStruct((b, n1, FFT_INNER, width), BF16)
  z_spec = pl.BlockSpec((None, FFT_TILE, FFT_INNER, tc), lambda bi, i, hh: (bi, i, 0, hh))
  tw_spec = pl.BlockSpec((FFT_TILE, FFT_INNER, LANES), lambda bi, i, hh: (i, 0, 0))
  return pl.pallas_call(
      _fft_stage1_kernel,
      grid=(b, n1 // FFT_TILE, width // tc),
      in_specs=[
          pl.BlockSpec((None, FFT_INNER, FFT_TILE, tc),
                       lambda bi, i, hh: (bi, 0, i, u_col0 // tc + hh)),
          pl.BlockSpec((2 * FFT_INNER, FFT_INNER), lambda bi, i, hh: (0, 0)),
          tw_spec, tw_spec,
      ],
      out_specs=[z_spec, z_spec],
      out_shape=[z_shape, z_shape],
      compiler_params=_params(
          ("arbitrary", "arbitrary", "arbitrary"),
          3 * FFT_TILE * FFT_INNER * tc * 2 + 2 * FFT_TILE * FFT_INNER * LANES * 4,
          FFT_TILE * FFT_INNER * tc * 2 + 8 * 2 * FFT_INNER * tc * 4),
      name="fourier_stage1",
  )(u_view, f_inner, tw_cos, tw_sin)


def _fft_stage2_kernel(zr_ref, zi_ref, f_ref, fc_ref, wf_ref, o_ref, zrt_ref, zit_ref,
                       res_ref, *, scale):
  n1 = zr_ref.shape[0]
  groups = zr_ref.shape[2] // FOURIER_GROUP_DIM
  rows = FFT_TILE * n1
  zrt_ref[...] = pltpu.einshape("abc->bac", zr_ref[...])
  zit_ref[...] = pltpu.einshape("abc->bac", zi_ref[...])

  def outer_dft(t, carry):
    z = jnp.concatenate([zrt_ref[t], zit_ref[t]], axis=0)
    o = jnp.dot(f_ref[...], z, preferred_element_type=F32)
    zrt_ref[t] = o[:n1].astype(BF16)
    zit_ref[t] = o[n1:].astype(BF16)
    return carry

  lax.fori_loop(0, FFT_TILE, outer_dft, 0)
  for g in range(groups):
    gs = slice(g * FOURIER_GROUP_DIM, (g + 1) * FOURIER_GROUP_DIM)
    a = jnp.concatenate([zrt_ref[:, :, gs].reshape(rows, FOURIER_GROUP_DIM),
                         zit_ref[:, :, gs].reshape(rows, FOURIER_GROUP_DIM)], axis=1)
    f = jnp.dot(a, fc_ref[...], preferred_element_type=F32) * scale
    r = jnp.dot(f.astype(BF16), wf_ref[g], preferred_element_type=F32).astype(BF16)
    res_ref[:, :, gs] = r.reshape(FFT_TILE, n1, FOURIER_GROUP_DIM)
  o_ref[...] = pltpu.einshape("abc->bac", res_ref[...])


def _fft_stage2(zr, zi, f_outer, f_chan, wf, layer, seq):
  b, n1, _, width = zr.shape
  tc = 1024
  groups = tc // FOURIER_GROUP_DIM
  blk = pl.BlockSpec((None, n1, FFT_TILE, tc), lambda bi, j, hh: (bi, 0, j, hh))
  out = pl.pallas_call(
      functools.partial(_fft_stage2_kernel,
                        scale=1.0 / math.sqrt(seq * FOURIER_GROUP_DIM)),
      grid=(b, FFT_INNER // FFT_TILE, width // tc),
      in_specs=[
          blk, blk,
          pl.BlockSpec((2 * n1, 2 * n1), lambda bi, j, hh: (0, 0)),
          pl.BlockSpec((2 * FOURIER_GROUP_DIM, FOURIER_GROUP_DIM), lambda bi, j, hh: (0, 0)),
          pl.BlockSpec((None, groups, FOURIER_GROUP_DIM, FOURIER_GROUP_DIM),
                       lambda bi, j, hh: (layer, hh, 0, 0)),
      ],
      out_specs=blk,
      out_shape=jax.ShapeDtypeStruct((b, n1, FFT_INNER, width), BF16),
      scratch_shapes=[pltpu.VMEM((FFT_TILE, n1, tc), BF16)] * 3,
      compiler_params=_params(
          ("arbitrary", "arbitrary", "arbitrary"),
          3 * n1 * FFT_TILE * tc * 2 + 4 * n1 * n1 * 2 + tc * FOURIER_GROUP_DIM * 2,
          3 * n1 * FFT_TILE * tc * 2 + 8 * 2 * n1 * tc * 4),
      name="fourier_stage2",
  )(zr, zi, f_outer, f_chan, wf)
  return out.reshape(b * seq, width)


def _out_proj_kernel(a_ref, f_ref, ag_ref, fg_ref, w_ref, x_ref, mod_ref, o_ref,
                     mixed_ref, *, gate_row, row_chunk):
  wa = a_ref.shape[1]

  @pl.when(pl.program_id(1) == 0)
  def _():
    def body(r, carry):
      rows = pl.ds(pl.multiple_of(r * row_chunk, row_chunk), row_chunk)
      a = a_ref[rows, :].astype(F32)
      ya = a * lax.rsqrt(jnp.mean(a * a, axis=-1, keepdims=True) + EPS) * ag_ref[...]
      mixed_ref[rows, :wa] = ya.astype(BF16)
      f = f_ref[rows, :].astype(F32)
      yf = f * lax.rsqrt(jnp.mean(f * f, axis=-1, keepdims=True) + EPS) * fg_ref[...]
      mixed_ref[rows, wa:] = yf.astype(BF16)
      return carry
    lax.fori_loop(0, a_ref.shape[0] // row_chunk, body, 0)

  acc = jnp.dot(mixed_ref[...], w_ref[...], preferred_element_type=F32)
  o_ref[...] = x_ref[...] + mod_ref[gate_row:gate_row + 1, :] * acc


def _out_proj(a2, f2, ag, fg, w, layer, x2, mod3, seq, b_off, gate_row):
  m, wa = a2.shape
  wf = f2.shape[1]
  d = w.shape[2]
  tm = min(1024, seq)
  tn = 512
  return pl.pallas_call(
      functools.partial(_out_proj_kernel, gate_row=gate_row, row_chunk=128),
      grid=(m // tm, d // tn),
      in_specs=[
          pl.BlockSpec((tm, wa), lambda i, j: (i, 0)),
          pl.BlockSpec((tm, wf), lambda i, j: (i, 0)),
          pl.BlockSpec((1, wa), lambda i, j: (0, 0)),
          pl.BlockSpec((1, wf), lambda i, j: (0, 0)),
          pl.BlockSpec((None, wa + wf, tn), lambda i, j: (layer, 0, j)),
          pl.BlockSpec((tm, tn), lambda i, j: (i, j)),
          pl.BlockSpec((None, N_MOD, tn), lambda i, j: (b_off + (i * tm) // seq, 0, j)),
      ],
      out_specs=pl.BlockSpec((tm, tn), lambda i, j: (i, j)),
      out_shape=jax.ShapeDtypeStruct((m, d), F32),
      scratch_shapes=[pltpu.VMEM((tm, wa + wf), BF16)],
      compiler_params=_params(
          ("arbitrary", "arbitrary"),
          tm * (wa + wf) * 2 + (wa + wf) * tn * 2 + 2 * tm * tn * 4,
          tm * (wa + wf) * 2 + 2 * tm * tn * 4),
      name="out_proj",
  )(a2, f2, ag.reshape(1, wa), fg.reshape(1, wf), w, x2, mod3)


def _ffn_up_kernel(h_ref, wg_ref, wu_ref, o_ref):
  h = h_ref[...]
  g = jnp.dot(h, wg_ref[...], preferred_element_type=F32)
  u = jnp.dot(h, wu_ref[...], preferred_element_type=F32)
  o_ref[...] = (_silu(g) * u).astype(BF16)


def _ffn_up(h, wg, wu, layer):
  m, d = h.shape
  f = wg.shape[2]
  tm = min(2048, m)
  tf = 256
  assert m % tm == 0 and f % tf == 0
  return pl.pallas_call(
      _ffn_up_kernel,
      grid=(m // tm, f // tf),
      in_specs=[
          pl.BlockSpec((tm, d), lambda i, j: (i, 0)),
          pl.BlockSpec((None, d, tf), lambda i, j: (layer, 0, j)),
          pl.BlockSpec((None, d, tf), lambda i, j: (layer, 0, j)),
      ],
      out_specs=pl.BlockSpec((tm, tf), lambda i, j: (i, j)),
      out_shape=jax.ShapeDtypeStruct((m, f), BF16),
      compiler_params=_params(
          ("arbitrary", "arbitrary"),
          tm * d * 2 + 2 * d * tf * 2 + tm * tf * 2,
          4 * tm * tf * 4),
      name="ffn_up",
  )(h, wg, wu)


def _ffn_down_kernel(a_ref, w_ref, x_ref, mod_ref, o_ref, *, gate_row):
  acc = jnp.dot(a_ref[...], w_ref[...], preferred_element_type=F32)
  o_ref[...] = x_ref[...] + mod_ref[gate_row:gate_row + 1, :] * acc


def _ffn_down(act, w, layer, x2, mod3, seq, b_off, gate_row):
  m, f = act.shape
  d = w.shape[2]
  tm = min(512, seq)
  tn = 512
  assert m % tm == 0 and seq % tm == 0
  return pl.pallas_call(
      functools.partial(_ffn_down_kernel, gate_row=gate_row),
      grid=(m // tm, d // tn),
      in_specs=[
          pl.BlockSpec((tm, f), lambda i, j: (i, 0)),
          pl.BlockSpec((None, f, tn), lambda i, j: (layer, 0, j)),
          pl.BlockSpec((tm, tn), lambda i, j: (i, j)),
          pl.BlockSpec((None, N_MOD, tn), lambda i, j: (b_off + (i * tm) // seq, 0, j)),
      ],
      out_specs=pl.BlockSpec((tm, tn), lambda i, j: (i, j)),
      out_shape=jax.ShapeDtypeStruct((m, d), F32),
      compiler_params=_params(
          ("arbitrary", "arbitrary"),
          tm * f * 2 + f * tn * 2 + 2 * tm * tn * 4,
          2 * tm * tn * 4),
      name="ffn_down",
  )(act, w, x2, mod3)


def _rope_tables(seq):
  t = np.arange(seq)
  row = (t // GRID_W).astype(np.float32)
  col = (t % GRID_W).astype(np.float32)
  axis_dim = HEAD_DIM // 2
  inv_freq = (1.0 / (ROPE_THETA ** (np.arange(0, axis_dim, 2, dtype=np.float32) / axis_dim))
              ).astype(np.float32)
  ang_r = row[:, None] * inv_freq[None, :]
  ang_c = col[:, None] * inv_freq[None, :]
  cos = np.concatenate([np.cos(ang_r), np.cos(ang_r), np.cos(ang_c), np.cos(ang_c)], axis=1)
  sin = np.concatenate([-np.sin(ang_r), np.sin(ang_r), -np.sin(ang_c), np.sin(ang_c)], axis=1)
  return jnp.asarray(cos, F32), jnp.asarray(sin, F32)


def _trunk(x, mods, b_off, weights, final_g):
  (norm_mix_g, w_in, q_norm_g, k_norm_g, w_fourier, attn_out_g, fourier_out_g, w_out,
   norm_ffn_g, w_gate, w_up, w_down) = weights
  b, seq, d = x.shape
  depth = w_in.shape[0]
  attn_width = attn_out_g.shape[1]
  fourier_width = fourier_out_g.shape[1]
  kv_width = (w_in.shape[2] - attn_width - fourier_width) // 2
  cos, sin_signed = _rope_tables(seq)
  u_col0 = attn_width + 2 * kv_width
  dense_fourier = seq <= FOURIER_DENSE_MAX_SEQ
  if not dense_fourier:
    f_inner, tw_cos, tw_sin, f_outer, f_chan = _fourier_tables(seq)
  x2 = x.reshape(b * seq, d)
  for l in range(depth):
    mod3 = mods[l]
    h = _norm_mod(x2, norm_mix_g[l], mod3, seq, b_off, shift_row=0, scale_row=1)
    proj = _in_proj(h, w_in, l, cos, sin_signed, q_norm_g[l], k_norm_g[l], seq,
                    attn_width, kv_width)
    proj3 = proj.reshape(b, seq, proj.shape[1])
    a = _attention(proj3, attn_width, kv_width)
    if dense_fourier:
      fm = _fourier_dense(proj3, w_fourier, l, u_col0, fourier_width)
      fm = fm.reshape(b * seq, fourier_width)
    else:
      zr, zi = _fft_stage1(proj3, f_inner, tw_cos, tw_sin, u_col0, fourier_width)
      fm = _fft_stage2(zr, zi, f_outer, f_chan, w_fourier, l, seq)
    x2 = _out_proj(a.reshape(b * seq, attn_width), fm, attn_out_g[l], fourier_out_g[l],
                   w_out, l, x2, mod3, seq, b_off, gate_row=2)
    h = _norm_mod(x2, norm_ffn_g[l], mod3, seq, b_off, shift_row=3, scale_row=4)
    act = _ffn_up(h, w_gate, w_up, l)
    x2 = _ffn_down(act, w_down, l, x2, mod3, seq, b_off, gate_row=5)
  return _final_norm(x2, final_g).reshape(b, seq, d)


def kernel(x_prompt, x_sample, c_prompt, c_sample, ada_w, ada_b, norm_mix_g, w_in, q_norm_g,
           k_norm_g, w_fourier, attn_out_g, fourier_out_g, w_out, norm_ffn_g, w_gate, w_up,
           w_down, final_norm_g):
  depth, d, _ = ada_w.shape
  nb_prompt = c_prompt.shape[0]
  nb = nb_prompt + c_sample.shape[0]
  rows = -(-nb // 8) * 8
  c_all = jnp.concatenate(
      [c_prompt, c_sample, jnp.zeros((rows - nb, d), F32)], axis=0)
  mod_all = _adaln(c_all, ada_w, ada_b)
  mods = [mod_all[l].reshape(rows, N_MOD, d) for l in range(depth)]
  weights = (norm_mix_g, w_in.astype(BF16), q_norm_g, k_norm_g, w_fourier.astype(BF16),
             attn_out_g, fourier_out_g, w_out.astype(BF16), norm_ffn_g,
             w_gate.astype(BF16), w_up.astype(BF16), w_down.astype(BF16))
  y_prompt = _trunk(x_prompt, mods, 0, weights, final_norm_g)
  y_sample = _trunk(x_sample, mods, nb_prompt, weights, final_norm_g)
  return (y_prompt, y_sample)
```

```python
import functools
import math

import numpy as np
import jax
import jax.numpy as jnp
from jax import lax
from jax.experimental import pallas as pl
from jax.experimental.pallas import tpu as pltpu

F32 = jnp.float32
BF16 = jnp.bfloat16

HEAD_DIM = 128
GQA_GROUP = 4
FOURIER_GROUP_DIM = 128
N_MOD = 6
GRID_W = 64
ROPE_THETA = 10000.0
EPS = 1e-6
FFT_INNER = 128
FFT_TILE = 16
FOURIER_DENSE_MAX_SEQ = 2048

V7X_VMEM_BYTES = 64 * 1024 * 1024
V7X_VMEM_BUDGET = 56 * 1024 * 1024
LANES = 128
SUBLANES = 8


def _vmem_limit(pipelined_bytes, resident_bytes):
  need = 2 * pipelined_bytes + resident_bytes
  return int(min(V7X_VMEM_BUDGET, max(need, 16 * 1024 * 1024)))


def _params(semantics, pipelined_bytes, resident_bytes):
  return pltpu.CompilerParams(
      dimension_semantics=semantics,
      vmem_limit_bytes=_vmem_limit(pipelined_bytes, resident_bytes))


def _silu(x):
  return x / (1.0 + jnp.exp(-x))


def _adaln_kernel(c_ref, w_ref, b_ref, o_ref):
  sc = _silu(c_ref[...])
  o_ref[...] = jnp.dot(sc, w_ref[...], preferred_element_type=F32) + b_ref[...]


def _adaln(c_all, ada_w, ada_b):
  depth, d, n = ada_w.shape
  rows = c_all.shape[0]
  tn = 512
  return pl.pallas_call(
      _adaln_kernel,
      grid=(depth, n // tn),
      in_specs=[
          pl.BlockSpec((rows, d), lambda l, j: (0, 0)),
          pl.BlockSpec((None, d, tn), lambda l, j: (l, 0, j)),
          pl.BlockSpec((None, 1, tn), lambda l, j: (l, 0, j)),
      ],
      out_specs=pl.BlockSpec((None, rows, tn), lambda l, j: (l, 0, j)),
      out_shape=jax.ShapeDtypeStruct((depth, rows, n), F32),
      compiler_params=_params(("arbitrary", "arbitrary"), d * tn * 4, rows * d * 8),
      name="adaln_mod",
  )(c_all, ada_w, ada_b.reshape(depth, 1, n))


def _norm_mod_kernel(x_ref, g_ref, mod_ref, o_ref, *, shift_row, scale_row):
  x = x_ref[...]
  ms = jnp.mean(x * x, axis=-1, keepdims=True)
  y = x * lax.rsqrt(ms + EPS) * g_ref[...]
  scale = mod_ref[scale_row:scale_row + 1, :]
  shift = mod_ref[shift_row:shift_row + 1, :]
  o_ref[...] = (y * (1.0 + scale) + shift).astype(BF16)


def _norm_mod(x2, g, mod3, seq, b_off, shift_row, scale_row):
  m, d = x2.shape
  tr = 256
  return pl.pallas_call(
      functools.partial(_norm_mod_kernel, shift_row=shift_row, scale_row=scale_row),
      grid=(m // tr,),
      in_specs=[
          pl.BlockSpec((tr, d), lambda i: (i, 0)),
          pl.BlockSpec((1, d), lambda i: (0, 0)),
          pl.BlockSpec((None, N_MOD, d), lambda i: (b_off + (i * tr) // seq, 0, 0)),
      ],
      out_specs=pl.BlockSpec((tr, d), lambda i: (i, 0)),
      out_shape=jax.ShapeDtypeStruct((m, d), BF16),
      compiler_params=_params(("arbitrary",), tr * d * 6, 4 * tr * d * 4),
      name="norm_mod",
  )(x2, g.reshape(1, d), mod3)


def _final_norm_kernel(x_ref, g_ref, o_ref):
  x = x_ref[...]
  ms = jnp.mean(x * x, axis=-1, keepdims=True)
  o_ref[...] = x * lax.rsqrt(ms + EPS) * g_ref[...]


def _final_norm(x2, g):
  m, d = x2.shape
  tr = 256
  return pl.pallas_call(
      _final_norm_kernel,
      grid=(m // tr,),
      in_specs=[
          pl.BlockSpec((tr, d), lambda i: (i, 0)),
          pl.BlockSpec((1, d), lambda i: (0, 0)),
      ],
      out_specs=pl.BlockSpec((tr, d), lambda i: (i, 0)),
      out_shape=jax.ShapeDtypeStruct((m, d), F32),
      compiler_params=_params(("arbitrary",), tr * d * 8, 3 * tr * d * 4),
      name="final_norm",
  )(x2, g.reshape(1, d))


def _head_norm_rope(t, g, cos, sin_signed):
  ms = jnp.mean(t * t, axis=-1, keepdims=True)
  y = t * lax.rsqrt(ms + EPS) * g
  lane = lax.broadcasted_iota(jnp.int32, y.shape, 1)
  first_half = (lane & (HEAD_DIM // 4)) == 0
  partner = jnp.where(first_half,
                      pltpu.roll(y, HEAD_DIM - HEAD_DIM // 4, 1),
                      pltpu.roll(y, HEAD_DIM // 4, 1))
  return y * cos + partner * sin_signed


def _in_proj_kernel(h_ref, w_ref, cos_ref, sin_ref, qg_ref, kg_ref, o_ref, *,
                    n_q_blocks, q_scale):
  j = pl.program_id(1)
  acc = jnp.dot(h_ref[...], w_ref[...], preferred_element_type=F32)
  heads = acc.shape[1] // HEAD_DIM

  @pl.when(j < n_q_blocks)
  def _():
    for c in range(heads):
      sl = slice(c * HEAD_DIM, (c + 1) * HEAD_DIM)
      r = _head_norm_rope(acc[:, sl], qg_ref[...], cos_ref[...], sin_ref[...])
      o_ref[:, sl] = (r * q_scale).astype(BF16)

  @pl.when(j == n_q_blocks)
  def _():
    for c in range(heads):
      sl = slice(c * HEAD_DIM, (c + 1) * HEAD_DIM)
      r = _head_norm_rope(acc[:, sl], kg_ref[...], cos_ref[...], sin_ref[...])
      o_ref[:, sl] = r.astype(BF16)

  @pl.when(j > n_q_blocks)
  def _():
    o_ref[...] = acc.astype(BF16)


def _in_proj(h, w, layer, cos, sin_signed, qg, kg, seq, attn_width, kv_width):
  m, d = h.shape
  n = w.shape[2]
  tm = min(1024, seq)
  tn = kv_width
  assert attn_width % tn == 0 and seq % tm == 0
  blocks_per_seq = seq // tm
  return pl.pallas_call(
      functools.partial(_in_proj_kernel, n_q_blocks=attn_width // tn,
                        q_scale=HEAD_DIM ** -0.5 * math.log2(math.e)),
      grid=(m // tm, n // tn),
      in_specs=[
          pl.BlockSpec((tm, d), lambda i, j: (i, 0)),
          pl.BlockSpec((None, d, tn), lambda i, j: (layer, 0, j)),
          pl.BlockSpec((tm, HEAD_DIM), lambda i, j: (i % blocks_per_seq, 0)),
          pl.BlockSpec((tm, HEAD_DIM), lambda i, j: (i % blocks_per_seq, 0)),
          pl.BlockSpec((1, HEAD_DIM), lambda i, j: (0, 0)),
          pl.BlockSpec((1, HEAD_DIM), lambda i, j: (0, 0)),
      ],
      out_specs=pl.BlockSpec((tm, tn), lambda i, j: (i, j)),
      out_shape=jax.ShapeDtypeStruct((m, n), BF16),
      compiler_params=_params(
          ("arbitrary", "arbitrary"),
          tm * d * 2 + d * tn * 2 + tm * tn * 2 + 2 * tm * HEAD_DIM * 4,
          3 * tm * tn * 4),
      name="in_proj",
  )(h, w, cos, sin_signed, qg.reshape(1, HEAD_DIM), kg.reshape(1, HEAD_DIM))


def _attn_kernel(q_ref, k_ref, v_ref, o_ref, q4_ref, vt_ref, s0_ref, s1_ref, p0_ref,
                 p1_ref, m_ref, l_ref, acc_ref, *, tq, tk, vt_chunk):
  seq = k_ref.shape[0]
  nk = seq // tk
  assert nk % 2 == 0 and nk >= 2

  @pl.when(pl.program_id(2) == 0)
  def _():
    for c in range(seq // vt_chunk):
      sl = slice(c * vt_chunk, (c + 1) * vt_chunk)
      vt_ref[:, sl] = v_ref[sl, :].astype(F32).T.astype(BF16)

  for g in range(GQA_GROUP):
    q4_ref[g * tq:(g + 1) * tq, :] = q_ref[:, g * HEAD_DIM:(g + 1) * HEAD_DIM]
  m_ref[...] = jnp.full(m_ref.shape, -jnp.inf, F32)
  l_ref[...] = jnp.zeros(l_ref.shape, F32)
  acc_ref[...] = jnp.zeros(acc_ref.shape, F32)

  def scores(c, s_ref):
    ks = k_ref[pl.ds(pl.multiple_of(c * tk, tk), tk), :]
    s_ref[...] = lax.dot_general(ks, q4_ref[...], (((1,), (1,)), ((), ())),
                                 preferred_element_type=F32)

  def softmax(s_ref, p_ref):
    col_max = s_ref[0:SUBLANES, :]
    for r in range(1, tk // SUBLANES):
      col_max = jnp.maximum(col_max, s_ref[r * SUBLANES:(r + 1) * SUBLANES, :])
    m_old = m_ref[...]
    m_new = jnp.maximum(m_old, jnp.max(col_max, axis=0, keepdims=True))
    alpha = jnp.exp2(m_old - m_new)
    col_sum = jnp.zeros((SUBLANES, s_ref.shape[1]), F32)
    strip = 2 * SUBLANES
    for r in range(tk // strip):
      rows = slice(r * strip, (r + 1) * strip)
      p = jnp.exp2(s_ref[rows, :] - m_new)
      col_sum = col_sum + p[:SUBLANES] + p[SUBLANES:]
      p_ref[rows, :] = p.astype(BF16)
    l_ref[...] = alpha * l_ref[...] + jnp.sum(col_sum, axis=0, keepdims=True)
    m_ref[...] = m_new
    return alpha

  def weighted_values(c, p_ref, alpha):
    vt = vt_ref[:, pl.ds(pl.multiple_of(c * tk, tk), tk)]
    pv = jnp.dot(vt, p_ref[...], preferred_element_type=F32)
    acc_ref[...] = alpha * acc_ref[...] + pv

  scores(0, s0_ref)
  scores(1, s1_ref)
  alpha0 = softmax(s0_ref, p0_ref)

  def body(t, alpha_even):
    c = 2 * t + 1
    scores(c + 1, s0_ref)
    alpha_odd = softmax(s1_ref, p1_ref)
    weighted_values(c - 1, p0_ref, alpha_even)
    scores(c + 2, s1_ref)
    alpha_next = softmax(s0_ref, p0_ref)
    weighted_values(c, p1_ref, alpha_odd)
    return alpha_next

  alpha_even = lax.fori_loop(0, nk // 2 - 1, body, alpha0)
  alpha_odd = softmax(s1_ref, p1_ref)
  weighted_values(nk - 2, p0_ref, alpha_even)
  weighted_values(nk - 1, p1_ref, alpha_odd)
  out = acc_ref[...] / l_ref[...]
  for g in range(GQA_GROUP):
    o_ref[:, g * HEAD_DIM:(g + 1) * HEAD_DIM] = (
        out[:, g * tq:(g + 1) * tq].T.astype(BF16))


def _attention(proj3, attn_width, kv_width):
  b, seq, _ = proj3.shape
  n_kv = kv_width // HEAD_DIM
  tq = min(512, seq)
  tk = min(1024, seq // 2)
  vt_chunk = min(512, seq)
  n_lanes = GQA_GROUP * tq
  gw = GQA_GROUP * HEAD_DIM
  k_col0 = attn_width // HEAD_DIM
  v_col0 = (attn_width + kv_width) // HEAD_DIM
  return pl.pallas_call(
      functools.partial(_attn_kernel, tq=tq, tk=tk, vt_chunk=vt_chunk),
      grid=(b, n_kv, seq // tq),
      in_specs=[
          pl.BlockSpec((None, tq, gw), lambda bi, h, i: (bi, i, h)),
          pl.BlockSpec((None, seq, HEAD_DIM), lambda bi, h, i: (bi, 0, k_col0 + h)),
          pl.BlockSpec((None, seq, HEAD_DIM), lambda bi, h, i: (bi, 0, v_col0 + h)),
      ],
      out_specs=pl.BlockSpec((None, tq, gw), lambda bi, h, i: (bi, i, h)),
      out_shape=jax.ShapeDtypeStruct((b, seq, attn_width), BF16),
      scratch_shapes=[
          pltpu.VMEM((n_lanes, HEAD_DIM), BF16),
          pltpu.VMEM((HEAD_DIM, seq), BF16),
          pltpu.VMEM((tk, n_lanes), F32),
          pltpu.VMEM((tk, n_lanes), F32),
          pltpu.VMEM((tk, n_lanes), BF16),
          pltpu.VMEM((tk, n_lanes), BF16),
          pltpu.VMEM((1, n_lanes), F32),
          pltpu.VMEM((1, n_lanes), F32),
          pltpu.VMEM((HEAD_DIM, n_lanes), F32),
      ],
      compiler_params=_params(
          ("arbitrary", "arbitrary", "arbitrary"),
          2 * tq * gw * 2 + 2 * seq * HEAD_DIM * 2,
          seq * HEAD_DIM * 2 + 3 * tk * n_lanes * 4 + 4 * tk * n_lanes * 4),
      name="attention",
  )(proj3, proj3, proj3)


def _dft_cos_sin(n):
  idx = np.arange(n, dtype=np.int64)
  ang = 2.0 * np.pi * ((idx[:, None] * idx[None, :]) % n).astype(np.float64) / n
  return np.cos(ang), np.sin(ang)


def _angle_cos_sin(int_prod, n):
  ang = int_prod.astype(F32) * (2.0 * math.pi / n)
  return jnp.cos(ang), jnp.sin(ang)


def _fourier_tables(seq):
  n1 = seq // FFT_INNER
  c2, s2 = _dft_cos_sin(FFT_INNER)
  f_inner = np.concatenate([c2, -s2], axis=0)
  k2 = jnp.arange(FFT_INNER, dtype=jnp.int32)
  i1 = jnp.arange(n1, dtype=jnp.int32)
  tw_c, tw_s = _angle_cos_sin((i1[:, None] * k2[None, :]) % seq, seq)
  tw_cos = jnp.broadcast_to(tw_c[:, :, None], (n1, FFT_INNER, LANES))
  tw_sin = jnp.broadcast_to(tw_s[:, :, None], (n1, FFT_INNER, LANES))
  c1, s1 = _dft_cos_sin(n1)
  f_outer = np.block([[c1, s1], [-s1, c1]])
  cc, sc = _dft_cos_sin(FOURIER_GROUP_DIM)
  f_chan = np.concatenate([cc, sc], axis=0)
  return (jnp.asarray(f_inner, BF16), tw_cos, tw_sin,
          jnp.asarray(f_outer, BF16), jnp.asarray(f_chan, BF16))


def _fourier_dense_kernel(u_ref, fc_ref, cpos_ref, spos_ref, wf_ref, o_ref, uc_ref, us_ref,
                          *, scale):
  groups = u_ref.shape[1] // FOURIER_GROUP_DIM

  @pl.when(pl.program_id(2) == 0)
  def _():
    for g in range(groups):
      gs = slice(g * FOURIER_GROUP_DIM, (g + 1) * FOURIER_GROUP_DIM)
      y = jnp.dot(u_ref[:, gs], fc_ref[...], preferred_element_type=F32)
      uc_ref[:, gs] = y[:, :FOURIER_GROUP_DIM].astype(BF16)
      us_ref[:, gs] = y[:, FOURIER_GROUP_DIM:].astype(BF16)

  f = (jnp.dot(cpos_ref[...], uc_ref[...], preferred_element_type=F32)
       - jnp.dot(spos_ref[...], us_ref[...], preferred_element_type=F32))
  f = (f * scale).astype(BF16)
  for g in range(groups):
    gs = slice(g * FOURIER_GROUP_DIM, (g + 1) * FOURIER_GROUP_DIM)
    o_ref[:, gs] = jnp.dot(f[:, gs], wf_ref[g], preferred_element_type=F32).astype(BF16)


def _fourier_dense(proj3, wf, layer, u_col0, width):
  b, seq, _ = proj3.shape
  tn = 1024
  ts = min(512, seq)
  assert u_col0 % tn == 0 and width % tn == 0 and seq % ts == 0
  groups = tn // FOURIER_GROUP_DIM
  idx = jnp.arange(seq, dtype=jnp.int32)
  cpos, spos = _angle_cos_sin((idx[:, None] * idx[None, :]) % seq, seq)
  cc, sc = _dft_cos_sin(FOURIER_GROUP_DIM)
  f_chan = jnp.asarray(np.concatenate([cc, sc], axis=1), BF16)
  return pl.pallas_call(
      functools.partial(_fourier_dense_kernel,
                        scale=1.0 / math.sqrt(seq * FOURIER_GROUP_DIM)),
      grid=(b, width // tn, seq // ts),
      in_specs=[
          pl.BlockSpec((None, seq, tn), lambda bi, j, i: (bi, 0, u_col0 // tn + j)),
          pl.BlockSpec((FOURIER_GROUP_DIM, 2 * FOURIER_GROUP_DIM), lambda bi, j, i: (0, 0)),
          pl.BlockSpec((ts, seq), lambda bi, j, i: (i, 0)),
          pl.BlockSpec((ts, seq), lambda bi, j, i: (i, 0)),
          pl.BlockSpec((None, groups, FOURIER_GROUP_DIM, FOURIER_GROUP_DIM),
                       lambda bi, j, i: (layer, j, 0, 0)),
      ],
      out_specs=pl.BlockSpec((None, ts, tn), lambda bi, j, i: (bi, i, j)),
      out_shape=jax.ShapeDtypeStruct((b, seq, width), BF16),
      scratch_shapes=[pltpu.VMEM((seq, tn), BF16), pltpu.VMEM((seq, tn), BF16)],
      compiler_params=_params(
          ("arbitrary", "arbitrary", "arbitrary"),
          seq * tn * 2 + 2 * ts * seq * 2 + ts * tn * 2,
          2 * seq * tn * 2 + 2 * seq * 2 * FOURIER_GROUP_DIM * 4 + 3 * ts * tn * 4),
      name="fourier_dense",
  )(proj3, f_chan, cpos.astype(BF16), spos.astype(BF16), wf)


def _fft_stage1_kernel(u_ref, f_ref, tc_ref, ts_ref, zr_ref, zi_ref):
  x = pltpu.einshape("abc->bac", u_ref[...])
  reps = u_ref.shape[2] // LANES
  for t in range(FFT_TILE):
    y = jnp.dot(f_ref[...], x[t], preferred_element_type=F32)
    yr = y[:FFT_INNER]
    yi = y[FFT_INNER:]
    tc = jnp.tile(tc_ref[t], (1, reps))
    ts = jnp.tile(ts_ref[t], (1, reps))
    zr_ref[t] = (yr * tc + yi * ts).astype(BF16)
    zi_ref[t] = (yi * tc - yr * ts).astype(BF16)


def _fft_stage1(proj3, f_inner, tw_cos, tw_sin, u_col0, width):
  b, seq, n_proj = proj3.shape
  n1 = seq // FFT_INNER
  tc = 1024
  assert n1 % FFT_TILE == 0 and u_col0 % tc == 0 and width % tc == 0
  u_view = proj3.reshape(b, FFT_INNER, n1, n_proj)
  z_shape = jax.ShapeDtypeStruct((b, n1, FFT_INNER, width), BF16)
  z_spec = pl.BlockSpec((None, FFT_TILE, FFT_INNER, tc), lambda bi, i, hh: (bi, i, 0, hh))
  tw_spec = pl.BlockSpec((FFT_TILE, FFT_INNER, LANES), lambda bi, i, hh: (i, 0, 0))
  return pl.pallas_call(
      _fft_stage1_kernel,
      grid=(b, n1 // FFT_TILE, width // tc),
      in_specs=[
          pl.BlockSpec((None, FFT_INNER, FFT_TILE, tc),
                       lambda bi, i, hh: (bi, 0, i, u_col0 // tc + hh)),
          pl.BlockSpec((2 * FFT_INNER, FFT_INNER), lambda bi, i, hh: (0, 0)),
          tw_spec, tw_spec,
      ],
      out_specs=[z_spec, z_spec],
      out_shape=[z_shape, z_shape],
      compiler_params=_params(
          ("arbitrary", "arbitrary", "arbitrary"),
          3 * FFT_TILE * FFT_INNER * tc * 2 + 2 * FFT_TILE * FFT_INNER * LANES * 4,
          FFT_TILE * FFT_INNER * tc * 2 + 8 * 2 * FFT_INNER * tc * 4),
      name="fourier_stage1",
  )(u_view, f_inner, tw_cos, tw_sin)


def _fft_stage2_kernel(zr_ref, zi_ref, f_ref, fc_ref, wf_ref, o_ref, zrt_ref, zit_ref,
                       res_ref, *, scale):
  n1 = zr_ref.shape[0]
  groups = zr_ref.shape[2] // FOURIER_GROUP_DIM
  rows = FFT_TILE * n1
  zrt_ref[...] = pltpu.einshape("abc->bac", zr_ref[...])
  zit_ref[...] = pltpu.einshape("abc->bac", zi_ref[...])

  def outer_dft(t, carry):
    z = jnp.concatenate([zrt_ref[t], zit_ref[t]], axis=0)
    o = jnp.dot(f_ref[...], z, preferred_element_type=F32)
    zrt_ref[t] = o[:n1].astype(BF16)
    zit_ref[t] = o[n1:].astype(BF16)
    return carry

  lax.fori_loop(0, FFT_TILE, outer_dft, 0)
  for g in range(groups):
    gs = slice(g * FOURIER_GROUP_DIM, (g + 1) * FOURIER_GROUP_DIM)
    a = jnp.concatenate([zrt_ref[:, :, gs].reshape(rows, FOURIER_GROUP_DIM),
                         zit_ref[:, :, gs].reshape(rows, FOURIER_GROUP_DIM)], axis=1)
    f = jnp.dot(a, fc_ref[...], preferred_element_type=F32) * scale
    r = jnp.dot(f.astype(BF16), wf_ref[g], preferred_element_type=F32).astype(BF16)
    res_ref[:, :, gs] = r.reshape(FFT_TILE, n1, FOURIER_GROUP_DIM)
  o_ref[...] = pltpu.einshape("abc->bac", res_ref[...])


def _fft_stage2(zr, zi, f_outer, f_chan, wf, layer, seq):
  b, n1, _, width = zr.shape
  tc = 1024
  groups = tc // FOURIER_GROUP_DIM
  blk = pl.BlockSpec((None, n1, FFT_TILE, tc), lambda bi, j, hh: (bi, 0, j, hh))
  out = pl.pallas_call(
      functools.partial(_fft_stage2_kernel,
                        scale=1.0 / math.sqrt(seq * FOURIER_GROUP_DIM)),
      grid=(b, FFT_INNER // FFT_TILE, width // tc),
      in_specs=[
          blk, blk,
          pl.BlockSpec((2 * n1, 2 * n1), lambda bi, j, hh: (0, 0)),
          pl.BlockSpec((2 * FOURIER_GROUP_DIM, FOURIER_GROUP_DIM), lambda bi, j, hh: (0, 0)),
          pl.BlockSpec((None, groups, FOURIER_GROUP_DIM, FOURIER_GROUP_DIM),
                       lambda bi, j, hh: (layer, hh, 0, 0)),
      ],
      out_specs=blk,
      out_shape=jax.ShapeDtypeStruct((b, n1, FFT_INNER, width), BF16),
      scratch_shapes=[pltpu.VMEM((FFT_TILE, n1, tc), BF16)] * 3,
      compiler_params=_params(
          ("arbitrary", "arbitrary", "arbitrary"),
          3 * n1 * FFT_TILE * tc * 2 + 4 * n1 * n1 * 2 + tc * FOURIER_GROUP_DIM * 2,
          3 * n1 * FFT_TILE * tc * 2 + 8 * 2 * n1 * tc * 4),
      name="fourier_stage2",
  )(zr, zi, f_outer, f_chan, wf)
  return out.reshape(b * seq, width)


def _out_proj_kernel(a_ref, f_ref, ag_ref, fg_ref, w_ref, x_ref, mod_ref, o_ref,
                     mixed_ref, *, gate_row, row_chunk):
  wa = a_ref.shape[1]

  @pl.when(pl.program_id(1) == 0)
  def _():
    def body(r, carry):
      rows = pl.ds(pl.multiple_of(r * row_chunk, row_chunk), row_chunk)
      a = a_ref[rows, :].astype(F32)
      ya = a * lax.rsqrt(jnp.mean(a * a, axis=-1, keepdims=True) + EPS) * ag_ref[...]
      mixed_ref[rows, :wa] = ya.astype(BF16)
      f = f_ref[rows, :].astype(F32)
      yf = f * lax.rsqrt(jnp.mean(f * f, axis=-1, keepdims=True) + EPS) * fg_ref[...]
      mixed_ref[rows, wa:] = yf.astype(BF16)
      return carry
    lax.fori_loop(0, a_ref.shape[0] // row_chunk, body, 0)

  acc = jnp.dot(mixed_ref[...], w_ref[...], preferred_element_type=F32)
  o_ref[...] = x_ref[...] + mod_ref[gate_row:gate_row + 1, :] * acc


def _out_proj(a2, f2, ag, fg, w, layer, x2, mod3, seq, b_off, gate_row):
  m, wa = a2.shape
  wf = f2.shape[1]
  d = w.shape[2]
  tm = min(1024, seq)
  tn = 512
  return pl.pallas_call(
      functools.partial(_out_proj_kernel, gate_row=gate_row, row_chunk=128),
      grid=(m // tm, d // tn),
      in_specs=[
          pl.BlockSpec((tm, wa), lambda i, j: (i, 0)),
          pl.BlockSpec((tm, wf), lambda i, j: (i, 0)),
          pl.BlockSpec((1, wa), lambda i, j: (0, 0)),
          pl.BlockSpec((1, wf), lambda i, j: (0, 0)),
          pl.BlockSpec((None, wa + wf, tn), lambda i, j: (layer, 0, j)),
          pl.BlockSpec((tm, tn), lambda i, j: (i, j)),
          pl.BlockSpec((None, N_MOD, tn), lambda i, j: (b_off + (i * tm) // seq, 0, j)),
      ],
      out_specs=pl.BlockSpec((tm, tn), lambda i, j: (i, j)),
      out_shape=jax.ShapeDtypeStruct((m, d), F32),
      scratch_shapes=[pltpu.VMEM((tm, wa + wf), BF16)],
      compiler_params=_params(
          ("arbitrary", "arbitrary"),
          tm * (wa + wf) * 2 + (wa + wf) * tn * 2 + 2 * tm * tn * 4,
          tm * (wa + wf) * 2 + 2 * tm * tn * 4),
      name="out_proj",
  )(a2, f2, ag.reshape(1, wa), fg.reshape(1, wf), w, x2, mod3)


def _ffn_up_kernel(h_ref, wg_ref, wu_ref, o_ref):
  h = h_ref[...]
  g = jnp.dot(h, wg_ref[...], preferred_element_type=F32)
  u = jnp.dot(h, wu_ref[...], preferred_element_type=F32)
  o_ref[...] = (_silu(g) * u).astype(BF16)


def _ffn_up(h, wg, wu, layer):
  m, d = h.shape
  f = wg.shape[2]
  tm = min(2048, m)
  tf = 256
  assert m % tm == 0 and f % tf == 0
  return pl.pallas_call(
      _ffn_up_kernel,
      grid=(m // tm, f // tf),
      in_specs=[
          pl.BlockSpec((tm, d), lambda i, j: (i, 0)),
          pl.BlockSpec((None, d, tf), lambda i, j: (layer, 0, j)),
          pl.BlockSpec((None, d, tf), lambda i, j: (layer, 0, j)),
      ],
      out_specs=pl.BlockSpec((tm, tf), lambda i, j: (i, j)),
      out_shape=jax.ShapeDtypeStruct((m, f), BF16),
      compiler_params=_params(
          ("arbitrary", "arbitrary"),
          tm * d * 2 + 2 * d * tf * 2 + tm * tf * 2,
          4 * tm * tf * 4),
      name="ffn_up",
  )(h, wg, wu)


def _ffn_down_kernel(a_ref, w_ref, x_ref, mod_ref, o_ref, *, gate_row):
  acc = jnp.dot(a_ref[...], w_ref[...], preferred_element_type=F32)
  o_ref[...] = x_ref[...] + mod_ref[gate_row:gate_row + 1, :] * acc


def _ffn_down(act, w, layer, x2, mod3, seq, b_off, gate_row):
  m, f = act.shape
  d = w.shape[2]
  tm = min(512, seq)
  tn = 512
  assert m % tm == 0 and seq % tm == 0
  return pl.pallas_call(
      functools.partial(_ffn_down_kernel, gate_row=gate_row),
      grid=(m // tm, d // tn),
      in_specs=[
          pl.BlockSpec((tm, f), lambda i, j: (i, 0)),
          pl.BlockSpec((None, f, tn), lambda i, j: (layer, 0, j)),
          pl.BlockSpec((tm, tn), lambda i, j: (i, j)),
          pl.BlockSpec((None, N_MOD, tn), lambda i, j: (b_off + (i * tm) // seq, 0, j)),
      ],
      out_specs=pl.BlockSpec((tm, tn), lambda i, j: (i, j)),
      out_shape=jax.ShapeDtypeStruct((m, d), F32),
      compiler_params=_params(
          ("arbitrary", "arbitrary"),
          tm * f * 2 + f * tn * 2 + 2 * tm * tn * 4,
          2 * tm * tn * 4),
      name="ffn_down",
  )(act, w, x2, mod3)


def _rope_tables(seq):
  t = np.arange(seq)
  row = (t // GRID_W).astype(np.float32)
  col = (t % GRID_W).astype(np.float32)
  axis_dim = HEAD_DIM // 2
  inv_freq = (1.0 / (ROPE_THETA ** (np.arange(0, axis_dim, 2, dtype=np.float32) / axis_dim))
              ).astype(np.float32)
  ang_r = row[:, None] * inv_freq[None, :]
  ang_c = col[:, None] * inv_freq[None, :]
  cos = np.concatenate([np.cos(ang_r), np.cos(ang_r), np.cos(ang_c), np.cos(ang_c)], axis=1)
  sin = np.concatenate([-np.sin(ang_r), np.sin(ang_r), -np.sin(ang_c), np.sin(ang_c)], axis=1)
  return jnp.asarray(cos, F32), jnp.asarray(sin, F32)


def _trunk(x, mods, b_off, weights, final_g):
  (norm_mix_g, w_in, q_norm_g, k_norm_g, w_fourier, attn_out_g, fourier_out_g, w_out,
   norm_ffn_g, w_gate, w_up, w_down) = weights
  b, seq, d = x.shape
  depth = w_in.shape[0]
  attn_width = attn_out_g.shape[1]
  fourier_width = fourier_out_g.shape[1]
  kv_width = (w_in.shape[2] - attn_width - fourier_width) // 2
  cos, sin_signed = _rope_tables(seq)
  u_col0 = attn_width + 2 * kv_width
  dense_fourier = seq <= FOURIER_DENSE_MAX_SEQ
  if not dense_fourier:
    f_inner, tw_cos, tw_sin, f_outer, f_chan = _fourier_tables(seq)
  x2 = x.reshape(b * seq, d)
  for l in range(depth):
    mod3 = mods[l]
    h = _norm_mod(x2, norm_mix_g[l], mod3, seq, b_off, shift_row=0, scale_row=1)
    proj = _in_proj(h, w_in, l, cos, sin_signed, q_norm_g[l], k_norm_g[l], seq,
                    attn_width, kv_width)
    proj3 = proj.reshape(b, seq, proj.shape[1])
    a = _attention(proj3, attn_width, kv_width)
    if dense_fourier:
      fm = _fourier_dense(proj3, w_fourier, l, u_col0, fourier_width)
      fm = fm.reshape(b * seq, fourier_width)
    else:
      zr, zi = _fft_stage1(proj3, f_inner, tw_cos, tw_sin, u_col0, fourier_width)
      fm = _fft_stage2(zr, zi, f_outer, f_chan, w_fourier, l, seq)
    x2 = _out_proj(a.reshape(b * seq, attn_width), fm, attn_out_g[l], fourier_out_g[l],
                   w_out, l, x2, mod3, seq, b_off, gate_row=2)
    h = _norm_mod(x2, norm_ffn_g[l], mod3, seq, b_off, shift_row=3, scale_row=4)
    act = _ffn_up(h, w_gate, w_up, l)
    x2 = _ffn_down(act, w_down, l, x2, mod3, seq, b_off, gate_row=5)
  return _final_norm(x2, final_g).reshape(b, seq, d)


def kernel(x_prompt, x_sample, c_prompt, c_sample, ada_w, ada_b, norm_mix_g, w_in, q_norm_g,
           k_norm_g, w_fourier, attn_out_g, fourier_out_g, w_out, norm_ffn_g, w_gate, w_up,
           w_down, final_norm_g):
  depth, d, _ = ada_w.shape
  nb_prompt = c_prompt.shape[0]
  nb = nb_prompt + c_sample.shape[0]
  rows = -(-nb // 8) * 8
  c_all = jnp.concatenate(
      [c_prompt, c_sample, jnp.zeros((rows - nb, d), F32)], axis=0)
  mod_all = _adaln(c_all, ada_w, ada_b)
  mods = [mod_all[l].reshape(rows, N_MOD, d) for l in range(depth)]
  weights = (norm_mix_g, w_in.astype(BF16), q_norm_g, k_norm_g, w_fourier.astype(BF16),
             attn_out_g, fourier_out_g, w_out.astype(BF16), norm_ffn_g,
             w_gate.astype(BF16), w_up.astype(BF16), w_down.astype(BF16))
  y_prompt = _trunk(x_prompt, mods, 0, weights, final_norm_g)
  y_sample = _trunk(x_sample, mods, nb_prompt, weights, final_norm_g)
  return (y_prompt, y_sample)
```

```python
import functools
import math

import numpy as np
import jax
import jax.numpy as jnp
from jax import lax
from jax.experimental import pallas as pl
from jax.experimental.pallas import tpu as pltpu

F32 = jnp.float32
BF16 = jnp.bfloat16

HEAD_DIM = 128
GQA_GROUP = 4
FOURIER_GROUP_DIM = 128
N_MOD = 6
GRID_W = 64
ROPE_THETA = 10000.0
EPS = 1e-6
FFT_INNER = 128
FFT_TILE = 16
FOURIER_DENSE_MAX_SEQ = 2048

V7X_VMEM_BYTES = 64 * 1024 * 1024
V7X_VMEM_BUDGET = 56 * 1024 * 1024
LANES = 128
SUBLANES = 8


def _vmem_limit(pipelined_bytes, resident_bytes):
  need = 2 * pipelined_bytes + resident_bytes
  return int(min(V7X_VMEM_BUDGET, max(need, 16 * 1024 * 1024)))


def _params(semantics, pipelined_bytes, resident_bytes):
  return pltpu.CompilerParams(
      dimension_semantics=semantics,
      vmem_limit_bytes=_vmem_limit(pipelined_bytes, resident_bytes))


def _silu(x):
  return x / (1.0 + jnp.exp(-x))


def _adaln_kernel(c_ref, w_ref, b_ref, o_ref):
  sc = _silu(c_ref[...])
  o_ref[...] = jnp.dot(sc, w_ref[...], preferred_element_type=F32) + b_ref[...]


def _adaln(c_all, ada_w, ada_b):
  depth, d, n = ada_w.shape
  rows = c_all.shape[0]
  tn = 512
  return pl.pallas_call(
      _adaln_kernel,
      grid=(depth, n // tn),
      in_specs=[
          pl.BlockSpec((rows, d), lambda l, j: (0, 0)),
          pl.BlockSpec((None, d, tn), lambda l, j: (l, 0, j)),
          pl.BlockSpec((None, 1, tn), lambda l, j: (l, 0, j)),
      ],
      out_specs=pl.BlockSpec((None, rows, tn), lambda l, j: (l, 0, j)),
      out_shape=jax.ShapeDtypeStruct((depth, rows, n), F32),
      compiler_params=_params(("arbitrary", "arbitrary"), d * tn * 4, rows * d * 8),
      name="adaln_mod",
  )(c_all, ada_w, ada_b.reshape(depth, 1, n))


def _norm_mod_kernel(x_ref, g_ref, mod_ref, o_ref, *, shift_row, scale_row):
  x = x_ref[...]
  ms = jnp.mean(x * x, axis=-1, keepdims=True)
  y = x * lax.rsqrt(ms + EPS) * g_ref[...]
  scale = mod_ref[scale_row:scale_row + 1, :]
  shift = mod_ref[shift_row:shift_row + 1, :]
  o_ref[...] = (y * (1.0 + scale) + shift).astype(BF16)


def _norm_mod(x2, g, mod3, seq, b_off, shift_row, scale_row):
  m, d = x2.shape
  tr = 256
  return pl.pallas_call(
      functools.partial(_norm_mod_kernel, shift_row=shift_row, scale_row=scale_row),
      grid=(m // tr,),
      in_specs=[
          pl.BlockSpec((tr, d), lambda i: (i, 0)),
          pl.BlockSpec((1, d), lambda i: (0, 0)),
          pl.BlockSpec((None, N_MOD, d), lambda i: (b_off + (i * tr) // seq, 0, 0)),
      ],
      out_specs=pl.BlockSpec((tr, d), lambda i: (i, 0)),
      out_shape=jax.ShapeDtypeStruct((m, d), BF16),
      compiler_params=_params(("arbitrary",), tr * d * 6, 4 * tr * d * 4),
      name="norm_mod",
  )(x2, g.reshape(1, d), mod3)


def _final_norm_kernel(x_ref, g_ref, o_ref):
  x = x_ref[...]
  ms = jnp.mean(x * x, axis=-1, keepdims=True)
  o_ref[...] = x * lax.rsqrt(ms + EPS) * g_ref[...]


def _final_norm(x2, g):
  m, d = x2.shape
  tr = 256
  return pl.pallas_call(
      _final_norm_kernel,
      grid=(m // tr,),
      in_specs=[
          pl.BlockSpec((tr, d), lambda i: (i, 0)),
          pl.BlockSpec((1, d), lambda i: (0, 0)),
      ],
      out_specs=pl.BlockSpec((tr, d), lambda i: (i, 0)),
      out_shape=jax.ShapeDtypeStruct((m, d), F32),
      compiler_params=_params(("arbitrary",), tr * d * 8, 3 * tr * d * 4),
      name="final_norm",
  )(x2, g.reshape(1, d))


def _head_norm_rope(t, g, cos, sin_signed):
  ms = jnp.mean(t * t, axis=-1, keepdims=True)
  y = t * lax.rsqrt(ms + EPS) * g
  lane = lax.broadcasted_iota(jnp.int32, y.shape, 1)
  first_half = (lane & (HEAD_DIM // 4)) == 0
  partner = jnp.where(first_half,
                      pltpu.roll(y, HEAD_DIM - HEAD_DIM // 4, 1),
                      pltpu.roll(y, HEAD_DIM // 4, 1))
  return y * cos + partner * sin_signed


def _in_proj_kernel(h_ref, w_ref, cos_ref, sin_ref, qg_ref, kg_ref, o_ref, *,
                    n_q_blocks, q_scale):
  j = pl.program_id(1)
  acc = jnp.dot(h_ref[...], w_ref[...], preferred_element_type=F32)
  heads = acc.shape[1] // HEAD_DIM

  @pl.when(j < n_q_blocks)
  def _():
    for c in range(heads):
      sl = slice(c * HEAD_DIM, (c + 1) * HEAD_DIM)
      r = _head_norm_rope(acc[:, sl], qg_ref[...], cos_ref[...], sin_ref[...])
      o_ref[:, sl] = (r * q_scale).astype(BF16)

  @pl.when(j == n_q_blocks)
  def _():
    for c in range(heads):
      sl = slice(c * HEAD_DIM, (c + 1) * HEAD_DIM)
      r = _head_norm_rope(acc[:, sl], kg_ref[...], cos_ref[...], sin_ref[...])
      o_ref[:, sl] = r.astype(BF16)

  @pl.when(j > n_q_blocks)
  def _():
    o_ref[...] = acc.astype(BF16)


def _in_proj(h, w, layer, cos, sin_signed, qg, kg, seq, attn_width, kv_width):
  m, d = h.shape
  n = w.shape[2]
  tm = min(1024, seq)
  tn = kv_width
  assert attn_width % tn == 0 and seq % tm == 0
  blocks_per_seq = seq // tm
  return pl.pallas_call(
      functools.partial(_in_proj_kernel, n_q_blocks=attn_width // tn,
                        q_scale=HEAD_DIM ** -0.5 * math.log2(math.e)),
      grid=(m // tm, n // tn),
      in_specs=[
          pl.BlockSpec((tm, d), lambda i, j: (i, 0)),
          pl.BlockSpec((None, d, tn), lambda i, j: (layer, 0, j)),
          pl.BlockSpec((tm, HEAD_DIM), lambda i, j: (i % blocks_per_seq, 0)),
          pl.BlockSpec((tm, HEAD_DIM), lambda i, j: (i % blocks_per_seq, 0)),
          pl.BlockSpec((1, HEAD_DIM), lambda i, j: (0, 0)),
          pl.BlockSpec((1, HEAD_DIM), lambda i, j: (0, 0)),
      ],
      out_specs=pl.BlockSpec((tm, tn), lambda i, j: (i, j)),
      out_shape=jax.ShapeDtypeStruct((m, n), BF16),
      compiler_params=_params(
          ("arbitrary", "arbitrary"),
          tm * d * 2 + d * tn * 2 + tm * tn * 2 + 2 * tm * HEAD_DIM * 4,
          3 * tm * tn * 4),
      name="in_proj",
  )(h, w, cos, sin_signed, qg.reshape(1, HEAD_DIM), kg.reshape(1, HEAD_DIM))


def _attn_kernel(q_ref, k_ref, v_ref, o_ref, q4_ref, vt_ref, s0_ref, s1_ref, cm0_ref,
                 cm1_ref, p0_ref, p1_ref, m_ref, l_ref, acc_ref, *, tq, tk, vt_chunk):
  seq = k_ref.shape[0]
  nk = seq // tk
  assert nk % 2 == 0 and nk >= 2

  @pl.when(pl.program_id(2) == 0)
  def _():
    for c in range(seq // vt_chunk):
      sl = slice(c * vt_chunk, (c + 1) * vt_chunk)
      vt_ref[:, sl] = v_ref[sl, :].astype(F32).T.astype(BF16)

  for g in range(GQA_GROUP):
    q4_ref[g * tq:(g + 1) * tq, :] = q_ref[:, g * HEAD_DIM:(g + 1) * HEAD_DIM]
  m_ref[...] = jnp.full(m_ref.shape, -jnp.inf, F32)
  l_ref[...] = jnp.zeros(l_ref.shape, F32)
  acc_ref[...] = jnp.zeros(acc_ref.shape, F32)

  def scores(c, s_ref, cm_ref):
    ks = k_ref[pl.ds(pl.multiple_of(c * tk, tk), tk), :]
    s = lax.dot_general(ks, q4_ref[...], (((1,), (1,)), ((), ())),
                        preferred_element_type=F32)
    s_ref[...] = s
    col_max = s[0:SUBLANES]
    for r in range(1, tk // SUBLANES):
      col_max = jnp.maximum(col_max, s[r * SUBLANES:(r + 1) * SUBLANES])
    cm_ref[...] = col_max

  def softmax(s_ref, cm_ref, p_ref):
    m_old = m_ref[...]
    m_new = jnp.maximum(m_old, jnp.max(cm_ref[...], axis=0, keepdims=True))
    alpha = jnp.exp2(m_old - m_new)
    col_sum = jnp.zeros((SUBLANES, s_ref.shape[1]), F32)
    strip = 2 * SUBLANES
    for r in range(tk // strip):
      rows = slice(r * strip, (r + 1) * strip)
      p = jnp.exp2(s_ref[rows, :] - m_new)
      col_sum = col_sum + p[:SUBLANES] + p[SUBLANES:]
      p_ref[rows, :] = p.astype(BF16)
    l_ref[...] = alpha * l_ref[...] + jnp.sum(col_sum, axis=0, keepdims=True)
    m_ref[...] = m_new
    return alpha

  def weighted_values(c, p_ref, alpha):
    vt = vt_ref[:, pl.ds(pl.multiple_of(c * tk, tk), tk)]
    pv = jnp.dot(vt, p_ref[...], preferred_element_type=F32)
    acc_ref[...] = alpha * acc_ref[...] + pv

  scores(0, s0_ref, cm0_ref)
  scores(1, s1_ref, cm1_ref)
  alpha0 = softmax(s0_ref, cm0_ref, p0_ref)

  def body(t, alpha_even):
    c = 2 * t + 1
    scores(c + 1, s0_ref, cm0_ref)
    alpha_odd = softmax(s1_ref, cm1_ref, p1_ref)
    weighted_values(c - 1, p0_ref, alpha_even)
    scores(c + 2, s1_ref, cm1_ref)
    alpha_next = softmax(s0_ref, cm0_ref, p0_ref)
    weighted_values(c, p1_ref, alpha_odd)
    return alpha_next

  alpha_even = lax.fori_loop(0, nk // 2 - 1, body, alpha0)
  alpha_odd = softmax(s1_ref, cm1_ref, p1_ref)
  weighted_values(nk - 2, p0_ref, alpha_even)
  weighted_values(nk - 1, p1_ref, alpha_odd)
  out = acc_ref[...] / l_ref[...]
  for g in range(GQA_GROUP):
    o_ref[:, g * HEAD_DIM:(g + 1) * HEAD_DIM] = (
        out[:, g * tq:(g + 1) * tq].T.astype(BF16))


def _attention(proj3, attn_width, kv_width):
  b, seq, _ = proj3.shape
  n_kv = kv_width // HEAD_DIM
  tq = min(512, seq)
  tk = min(1024, seq // 2)
  vt_chunk = min(512, seq)
  n_lanes = GQA_GROUP * tq
  gw = GQA_GROUP * HEAD_DIM
  k_col0 = attn_width // HEAD_DIM
  v_col0 = (attn_width + kv_width) // HEAD_DIM
  return pl.pallas_call(
      functools.partial(_attn_kernel, tq=tq, tk=tk, vt_chunk=vt_chunk),
      grid=(b, n_kv, seq // tq),
      in_specs=[
          pl.BlockSpec((None, tq, gw), lambda bi, h, i: (bi, i, h)),
          pl.BlockSpec((None, seq, HEAD_DIM), lambda bi, h, i: (bi, 0, k_col0 + h)),
          pl.BlockSpec((None, seq, HEAD_DIM), lambda bi, h, i: (bi, 0, v_col0 + h)),
      ],
      out_specs=pl.BlockSpec((None, tq, gw), lambda bi, h, i: (bi, i, h)),
      out_shape=jax.ShapeDtypeStruct((b, seq, attn_width), BF16),
      scratch_shapes=[
          pltpu.VMEM((n_lanes, HEAD_DIM), BF16),
          pltpu.VMEM((HEAD_DIM, seq), BF16),
          pltpu.VMEM((tk, n_lanes), F32),
          pltpu.VMEM((tk, n_lanes), F32),
          pltpu.VMEM((SUBLANES, n_lanes), F32),
          pltpu.VMEM((SUBLANES, n_lanes), F32),
          pltpu.VMEM((tk, n_lanes), BF16),
          pltpu.VMEM((tk, n_lanes), BF16),
          pltpu.VMEM((1, n_lanes), F32),
          pltpu.VMEM((1, n_lanes), F32),
          pltpu.VMEM((HEAD_DIM, n_lanes), F32),
      ],
      compiler_params=_params(
          ("arbitrary", "arbitrary", "arbitrary"),
          2 * tq * gw * 2 + 2 * seq * HEAD_DIM * 2,
          seq * HEAD_DIM * 2 + 3 * tk * n_lanes * 4 + 4 * tk * n_lanes * 4),
      name="attention",
  )(proj3, proj3, proj3)


def _dft_cos_sin(n):
  idx = np.arange(n, dtype=np.int64)
  ang = 2.0 * np.pi * ((idx[:, None] * idx[None, :]) % n).astype(np.float64) / n
  return np.cos(ang), np.sin(ang)


def _angle_cos_sin(int_prod, n):
  ang = int_prod.astype(F32) * (2.0 * math.pi / n)
  return jnp.cos(ang), jnp.sin(ang)


def _fourier_tables(seq):
  n1 = seq // FFT_INNER
  c2, s2 = _dft_cos_sin(FFT_INNER)
  f_inner = np.concatenate([c2, -s2], axis=0)
  k2 = jnp.arange(FFT_INNER, dtype=jnp.int32)
  i1 = jnp.arange(n1, dtype=jnp.int32)
  tw_c, tw_s = _angle_cos_sin((i1[:, None] * k2[None, :]) % seq, seq)
  tw_cos = jnp.broadcast_to(tw_c[:, :, None], (n1, FFT_INNER, LANES))
  tw_sin = jnp.broadcast_to(tw_s[:, :, None], (n1, FFT_INNER, LANES))
  c1, s1 = _dft_cos_sin(n1)
  f_outer = np.block([[c1, s1], [-s1, c1]])
  cc, sc = _dft_cos_sin(FOURIER_GROUP_DIM)
  f_chan = np.concatenate([cc, sc], axis=0)
  return (jnp.asarray(f_inner, BF16), tw_cos, tw_sin,
          jnp.asarray(f_outer, BF16), jnp.asarray(f_chan, BF16))


def _fourier_dense_kernel(u_ref, fc_ref, cpos_ref, spos_ref, wf_ref, o_ref, uc_ref, us_ref,
                          *, scale):
  groups = u_ref.shape[1] // FOURIER_GROUP_DIM

  @pl.when(pl.program_id(2) == 0)
  def _():
    for g in range(groups):
      gs = slice(g * FOURIER_GROUP_DIM, (g + 1) * FOURIER_GROUP_DIM)
      y = jnp.dot(u_ref[:, gs], fc_ref[...], preferred_element_type=F32)
      uc_ref[:, gs] = y[:, :FOURIER_GROUP_DIM].astype(BF16)
      us_ref[:, gs] = y[:, FOURIER_GROUP_DIM:].astype(BF16)

  f = (jnp.dot(cpos_ref[...], uc_ref[...], preferred_element_type=F32)
       - jnp.dot(spos_ref[...], us_ref[...], preferred_element_type=F32))
  f = (f * scale).astype(BF16)
  for g in range(groups):
    gs = slice(g * FOURIER_GROUP_DIM, (g + 1) * FOURIER_GROUP_DIM)
    o_ref[:, gs] = jnp.dot(f[:, gs], wf_ref[g], preferred_element_type=F32).astype(BF16)


def _fourier_dense(proj3, wf, layer, u_col0, width):
  b, seq, _ = proj3.shape
  tn = 1024
  ts = min(512, seq)
  assert u_col0 % tn == 0 and width % tn == 0 and seq % ts == 0
  groups = tn // FOURIER_GROUP_DIM
  idx = jnp.arange(seq, dtype=jnp.int32)
  cpos, spos = _angle_cos_sin((idx[:, None] * idx[None, :]) % seq, seq)
  cc, sc = _dft_cos_sin(FOURIER_GROUP_DIM)
  f_chan = jnp.asarray(np.concatenate([cc, sc], axis=1), BF16)
  return pl.pallas_call(
      functools.partial(_fourier_dense_kernel,
                        scale=1.0 / math.sqrt(seq * FOURIER_GROUP_DIM)),
      grid=(b, width // tn, seq // ts),
      in_specs=[
          pl.BlockSpec((None, seq, tn), lambda bi, j, i: (bi, 0, u_col0 // tn + j)),
          pl.BlockSpec((FOURIER_GROUP_DIM, 2 * FOURIER_GROUP_DIM), lambda bi, j, i: (0, 0)),
          pl.BlockSpec((ts, seq), lambda bi, j, i: (i, 0)),
          pl.BlockSpec((ts, seq), lambda bi, j, i: (i, 0)),
          pl.BlockSpec((None, groups, FOURIER_GROUP_DIM, FOURIER_GROUP_DIM),
                       lambda bi, j, i: (layer, j, 0, 0)),
      ],
      out_specs=pl.BlockSpec((None, ts, tn), lambda bi, j, i: (bi, i, j)),
      out_shape=jax.ShapeDtypeStruct((b, seq, width), BF16),
      scratch_shapes=[pltpu.VMEM((seq, tn), BF16), pltpu.VMEM((seq, tn), BF16)],
      compiler_params=_params(
          ("arbitrary", "arbitrary", "arbitrary"),
          seq * tn * 2 + 2 * ts * seq * 2 + ts * tn * 2,
          2 * seq * tn * 2 + 2 * seq * 2 * FOURIER_GROUP_DIM * 4 + 3 * ts * tn * 4),
      name="fourier_dense",
  )(proj3, f_chan, cpos.astype(BF16), spos.astype(BF16), wf)


def _fft_stage1_kernel(u_ref, f_ref, tc_ref, ts_ref, zr_ref, zi_ref):
  x = pltpu.einshape("abc->bac", u_ref[...])
  reps = u_ref.shape[2] // LANES
  for t in range(FFT_TILE):
    y = jnp.dot(f_ref[...], x[t], preferred_element_type=F32)
    yr = y[:FFT_INNER]
    yi = y[FFT_INNER:]
    tc = jnp.tile(tc_ref[t], (1, reps))
    ts = jnp.tile(ts_ref[t], (1, reps))
    zr_ref[t] = (yr * tc + yi * ts).astype(BF16)
    zi_ref[t] = (yi * tc - yr * ts).astype(BF16)


def _fft_stage1(proj3, f_inner, tw_cos, tw_sin, u_col0, width):
  b, seq, n_proj = proj3.shape
  n1 = seq // FFT_INNER
  tc = 1024
  assert n1 % FFT_TILE == 0 and u_col0 % tc == 0 and width % tc == 0
  u_view = proj3.reshape(b, FFT_INNER, n1, n_proj)
  z_shape = jax.ShapeDtypeStruct((b, n1, FFT_INNER, width), BF16)
  z_spec = pl.BlockSpec((None, FFT_TILE, FFT_INNER, tc), lambda bi, i, hh: (bi, i, 0, hh))
  tw_spec = pl.BlockSpec((FFT_TILE, FFT_INNER, LANES), lambda bi, i, hh: (i, 0, 0))
  return pl.pallas_call(
      _fft_stage1_kernel,
      grid=(b, n1 // FFT_TILE, width // tc),
      in_specs=[
          pl.BlockSpec((None, FFT_INNER, FFT_TILE, tc),
                       lambda bi, i, hh: (bi, 0, i, u_col0 // tc + hh)),
          pl.BlockSpec((2 * FFT_INNER, FFT_INNER), lambda bi, i, hh: (0, 0)),
          tw_spec, tw_spec,
      ],
      out_specs=[z_spec, z_spec],
      out_shape=[z_shape, z_shape],
      compiler_params=_params(
          ("arbitrary", "arbitrary", "arbitrary"),
          3 * FFT_TILE * FFT_INNER * tc * 2 + 2 * FFT_TILE * FFT_INNER * LANES * 4,
          FFT_TILE * FFT_INNER * tc * 2 + 8 * 2 * FFT_INNER * tc * 4),
      name="fourier_stage1",
  )(u_view, f_inner, tw_cos, tw_sin)


def _fft_stage2_kernel(zr_ref, zi_ref, f_ref, fc_ref, wf_ref, o_ref, zrt_ref, zit_ref,
                       res_ref, *, scale):
  n1 = zr_ref.shape[0]
  groups = zr_ref.shape[2] // FOURIER_GROUP_DIM
  rows = FFT_TILE * n1
  zrt_ref[...] = pltpu.einshape("abc->bac", zr_ref[...])
  zit_ref[...] = pltpu.einshape("abc->bac", zi_ref[...])

  def outer_dft(t, carry):
    z = jnp.concatenate([zrt_ref[t], zit_ref[t]], axis=0)
    o = jnp.dot(f_ref[...], z, preferred_element_type=F32)
    zrt_ref[t] = o[:n1].astype(BF16)
    zit_ref[t] = o[n1:].astype(BF16)
    return carry

  lax.fori_loop(0, FFT_TILE, outer_dft, 0)
  for g in range(groups):
    gs = slice(g * FOURIER_GROUP_DIM, (g + 1) * FOURIER_GROUP_DIM)
    a = jnp.concatenate([zrt_ref[:, :, gs].reshape(rows, FOURIER_GROUP_DIM),
                         zit_ref[:, :, gs].reshape(rows, FOURIER_GROUP_DIM)], axis=1)
    f = jnp.dot(a, fc_ref[...], preferred_element_type=F32) * scale
    r = jnp.dot(f.astype(BF16), wf_ref[g], preferred_element_type=F32).astype(BF16)
    res_ref[:, :, gs] = r.reshape(FFT_TILE, n1, FOURIER_GROUP_DIM)
  o_ref[...] = pltpu.einshape("abc->bac", res_ref[...])


def _fft_stage2(zr, zi, f_outer, f_chan, wf, layer, seq):
  b, n1, _, width = zr.shape
  tc = 1024
  groups = tc // FOURIER_GROUP_DIM
  blk = pl.BlockSpec((None, n1, FFT_TILE, tc), lambda bi, j, hh: (bi, 0, j, hh))
  out = pl.pallas_call(
      functools.partial(_fft_stage2_kernel,
                        scale=1.0 / math.sqrt(seq * FOURIER_GROUP_DIM)),
      grid=(b, FFT_INNER // FFT_TILE, width // tc),
      in_specs=[
          blk, blk,
          pl.BlockSpec((2 * n1, 2 * n1), lambda bi, j, hh: (0, 0)),
          pl.BlockSpec((2 * FOURIER_GROUP_DIM, FOURIER_GROUP_DIM), lambda bi, j, hh: (0, 0)),
          pl.BlockSpec((None, groups, FOURIER_GROUP_DIM, FOURIER_GROUP_DIM),
                       lambda bi, j, hh: (layer, hh, 0, 0)),
      ],
      out_specs=blk,
      out_shape=jax.ShapeDtypeStruct((b, n1, FFT_INNER, width), BF16),
      scratch_shapes=[pltpu.VMEM((FFT_TILE, n1, tc), BF16)] * 3,
      compiler_params=_params(
          ("arbitrary", "arbitrary", "arbitrary"),
          3 * n1 * FFT_TILE * tc * 2 + 4 * n1 * n1 * 2 + tc * FOURIER_GROUP_DIM * 2,
          3 * n1 * FFT_TILE * tc * 2 + 8 * 2 * n1 * tc * 4),
      name="fourier_stage2",
  )(zr, zi, f_outer, f_chan, wf)
  return out.reshape(b * seq, width)


def _out_proj_kernel(a_ref, f_ref, ag_ref, fg_ref, w_ref, x_ref, mod_ref, o_ref,
                     mixed_ref, *, gate_row, row_chunk):
  wa = a_ref.shape[1]

  @pl.when(pl.program_id(1) == 0)
  def _():
    def body(r, carry):
      rows = pl.ds(pl.multiple_of(r * row_chunk, row_chunk), row_chunk)
      a = a_ref[rows, :].astype(F32)
      ya = a * lax.rsqrt(jnp.mean(a * a, axis=-1, keepdims=True) + EPS) * ag_ref[...]
      mixed_ref[rows, :wa] = ya.astype(BF16)
      f = f_ref[rows, :].astype(F32)
      yf = f * lax.rsqrt(jnp.mean(f * f, axis=-1, keepdims=True) + EPS) * fg_ref[...]
      mixed_ref[rows, wa:] = yf.astype(BF16)
      return carry
    lax.fori_loop(0, a_ref.shape[0] // row_chunk, body, 0)

  acc = jnp.dot(mixed_ref[...], w_ref[...], preferred_element_type=F32)
  o_ref[...] = x_ref[...] + mod_ref[gate_row:gate_row + 1, :] * acc


def _out_proj(a2, f2, ag, fg, w, layer, x2, mod3, seq, b_off, gate_row):
  m, wa = a2.shape
  wf = f2.shape[1]
  d = w.shape[2]
  tm = min(1024, seq)
  tn = 512
  return pl.pallas_call(
      functools.partial(_out_proj_kernel, gate_row=gate_row, row_chunk=128),
      grid=(m // tm, d // tn),
      in_specs=[
          pl.BlockSpec((tm, wa), lambda i, j: (i, 0)),
          pl.BlockSpec((tm, wf), lambda i, j: (i, 0)),
          pl.BlockSpec((1, wa), lambda i, j: (0, 0)),
          pl.BlockSpec((1, wf), lambda i, j: (0, 0)),
          pl.BlockSpec((None, wa + wf, tn), lambda i, j: (layer, 0, j)),
          pl.BlockSpec((tm, tn), lambda i, j: (i, j)),
          pl.BlockSpec((None, N_MOD, tn), lambda i, j: (b_off + (i * tm) // seq, 0, j)),
      ],
      out_specs=pl.BlockSpec((tm, tn), lambda i, j: (i, j)),
      out_shape=jax.ShapeDtypeStruct((m, d), F32),
      scratch_shapes=[pltpu.VMEM((tm, wa + wf), BF16)],
      compiler_params=_params(
          ("arbitrary", "arbitrary"),
          tm * (wa + wf) * 2 + (wa + wf) * tn * 2 + 2 * tm * tn * 4,
          tm * (wa + wf) * 2 + 2 * tm * tn * 4),
      name="out_proj",
  )(a2, f2, ag.reshape(1, wa), fg.reshape(1, wf), w, x2, mod3)


def _ffn_up_kernel(h_ref, wg_ref, wu_ref, o_ref):
  h = h_ref[...]
  g = jnp.dot(h, wg_ref[...].astype(BF16), preferred_element_type=F32)
  u = jnp.dot(h, wu_ref[...].astype(BF16), preferred_element_type=F32)
  o_ref[...] = (_silu(g) * u).astype(BF16)


def _ffn_up(h, wg, wu, layer):
  m, d = h.shape
  f = wg.shape[2]
  tm = min(1024, m)
  tf = 256
  assert m % tm == 0 and f % tf == 0
  return pl.pallas_call(
      _ffn_up_kernel,
      grid=(m // tm, f // tf),
      in_specs=[
          pl.BlockSpec((tm, d), lambda i, j: (i, 0)),
          pl.BlockSpec((None, d, tf), lambda i, j: (layer, 0, j)),
          pl.BlockSpec((None, d, tf), lambda i, j: (layer, 0, j)),
      ],
      out_specs=pl.BlockSpec((tm, tf), lambda i, j: (i, j)),
      out_shape=jax.ShapeDtypeStruct((m, f), BF16),
      compiler_params=_params(
          ("arbitrary", "arbitrary"),
          tm * d * 2 + 2 * d * tf * 4 + tm * tf * 2,
          4 * tm * tf * 4 + 2 * d * tf * 2),
      name="ffn_up",
  )(h, wg, wu)


def _ffn_down_kernel(a_ref, w_ref, x_ref, mod_ref, o_ref, *, gate_row):
  acc = jnp.dot(a_ref[...], w_ref[...], preferred_element_type=F32)
  o_ref[...] = x_ref[...] + mod_ref[gate_row:gate_row + 1, :] * acc


def _ffn_down(act, w, layer, x2, mod3, seq, b_off, gate_row):
  m, f = act.shape
  d = w.shape[2]
  tm = min(512, seq)
  tn = 512
  assert m % tm == 0 and seq % tm == 0
  return pl.pallas_call(
      functools.partial(_ffn_down_kernel, gate_row=gate_row),
      grid=(m // tm, d // tn),
      in_specs=[
          pl.BlockSpec((tm, f), lambda i, j: (i, 0)),
          pl.BlockSpec((None, f, tn), lambda i, j: (layer, 0, j)),
          pl.BlockSpec((tm, tn), lambda i, j: (i, j)),
          pl.BlockSpec((None, N_MOD, tn), lambda i, j: (b_off + (i * tm) // seq, 0, j)),
      ],
      out_specs=pl.BlockSpec((tm, tn), lambda i, j: (i, j)),
      out_shape=jax.ShapeDtypeStruct((m, d), F32),
      compiler_params=_params(
          ("arbitrary", "arbitrary"),
          tm * f * 2 + f * tn * 2 + 2 * tm * tn * 4,
          2 * tm * tn * 4),
      name="ffn_down",
  )(act, w, x2, mod3)


def _rope_tables(seq):
  t = np.arange(seq)
  row = (t // GRID_W).astype(np.float32)
  col = (t % GRID_W).astype(np.float32)
  axis_dim = HEAD_DIM // 2
  inv_freq = (1.0 / (ROPE_THETA ** (np.arange(0, axis_dim, 2, dtype=np.float32) / axis_dim))
              ).astype(np.float32)
  ang_r = row[:, None] * inv_freq[None, :]
  ang_c = col[:, None] * inv_freq[None, :]
  cos = np.concatenate([np.cos(ang_r), np.cos(ang_r), np.cos(ang_c), np.cos(ang_c)], axis=1)
  sin = np.concatenate([-np.sin(ang_r), np.sin(ang_r), -np.sin(ang_c), np.sin(ang_c)], axis=1)
  return jnp.asarray(cos, F32), jnp.asarray(sin, F32)


def _trunk(x, mods, b_off, weights, final_g):
  (norm_mix_g, w_in, q_norm_g, k_norm_g, w_fourier, attn_out_g, fourier_out_g, w_out,
   norm_ffn_g, w_gate, w_up, w_down) = weights
  b, seq, d = x.shape
  depth = w_in.shape[0]
  attn_width = attn_out_g.shape[1]
  fourier_width = fourier_out_g.shape[1]
  kv_width = (w_in.shape[2] - attn_width - fourier_width) // 2
  cos, sin_signed = _rope_tables(seq)
  u_col0 = attn_width + 2 * kv_width
  dense_fourier = seq <= FOURIER_DENSE_MAX_SEQ
  if not dense_fourier:
    f_inner, tw_cos, tw_sin, f_outer, f_chan = _fourier_tables(seq)
  x2 = x.reshape(b * seq, d)
  for l in range(depth):
    mod3 = mods[l]
    h = _norm_mod(x2, norm_mix_g[l], mod3, seq, b_off, shift_row=0, scale_row=1)
    proj = _in_proj(h, w_in, l, cos, sin_signed, q_norm_g[l], k_norm_g[l], seq,
                    attn_width, kv_width)
    proj3 = proj.reshape(b, seq, proj.shape[1])
    a = _attention(proj3, attn_width, kv_width)
    if dense_fourier:
      fm = _fourier_dense(proj3, w_fourier, l, u_col0, fourier_width)
      fm = fm.reshape(b * seq, fourier_width)
    else:
      zr, zi = _fft_stage1(proj3, f_inner, tw_cos, tw_sin, u_col0, fourier_width)
      fm = _fft_stage2(zr, zi, f_outer, f_chan, w_fourier, l, seq)
    x2 = _out_proj(a.reshape(b * seq, attn_width), fm, attn_out_g[l], fourier_out_g[l],
                   w_out, l, x2, mod3, seq, b_off, gate_row=2)
    h = _norm_mod(x2, norm_ffn_g[l], mod3, seq, b_off, shift_row=3, scale_row=4)
    act = _ffn_up(h, w_gate, w_up, l)
    x2 = _ffn_down(act, w_down, l, x2, mod3, seq, b_off, gate_row=5)
  return _final_norm(x2, final_g).reshape(b, seq, d)


def kernel(x_prompt, x_sample, c_prompt, c_sample, ada_w, ada_b, norm_mix_g, w_in, q_norm_g,
           k_norm_g, w_fourier, attn_out_g, fourier_out_g, w_out, norm_ffn_g, w_gate, w_up,
           w_down, final_norm_g):
  depth, d, _ = ada_w.shape
  nb_prompt = c_prompt.shape[0]
  nb = nb_prompt + c_sample.shape[0]
  rows = -(-nb // 8) * 8
  c_all = jnp.concatenate(
      [c_prompt, c_sample, jnp.zeros((rows - nb, d), F32)], axis=0)
  mod_all = _adaln(c_all, ada_w, ada_b)
  mods = [mod_all[l].reshape(rows, N_MOD, d) for l in range(depth)]
  weights = (norm_mix_g, w_in.astype(BF16), q_norm_g, k_norm_g, w_fourier.astype(BF16),
             attn_out_g, fourier_out_g, w_out.astype(BF16), norm_ffn_g,
             w_gate, w_up, w_down.astype(BF16))
  y_prompt = _trunk(x_prompt, mods, 0, weights, final_norm_g)
  y_sample = _trunk(x_sample, mods, nb_prompt, weights, final_norm_g)
  return (y_prompt, y_sample)
```
